```python
import jax, jax.numpy as jnp
from jax import lax
import numpy as np

D_MODEL = 4096
BATCH = 4
SEQ = 2048
DEPTH = 2

GRID_W = 64
CTX_LEN = 256
N_MIXERS = 2
N_HEADS = 32
N_KV_HEADS = 8
HEAD_DIM = D_MODEL // N_HEADS
GQA_GROUP = N_HEADS // N_KV_HEADS
QKV_DIM = (N_HEADS + 2 * N_KV_HEADS) * HEAD_DIM
WINDOW = 128
BLOCK = 128
ROPE_THETA = 10000.0
CONV_WIDTH = 3
N_EXPERTS = 16
N_GROUPS = 4
EXPERTS_PER_GROUP = N_EXPERTS // N_GROUPS
TOP_K = 2
D_FF_EXPERT = D_MODEL // 4
N_MOD = 6
N_ATTN_LAYERS = (DEPTH + 1) // 2
N_CONV_LAYERS = DEPTH // 2
EPS = 1e-6
NEG = -1e30

kernel_name = "hybrid_swa_shortconv_grouped_moe_dit"


def rmsnorm(x, g):
    xf = x.astype(jnp.float32)
    y = xf * lax.rsqrt(jnp.mean(xf * xf, axis=-1, keepdims=True) + EPS)
    return (y * g.astype(jnp.float32)).astype(x.dtype)


def modulate(h, shift, scale):
    return h * (1 + scale) + shift


def rope_2d(n_tokens, dtype):
    rows = n_tokens // GRID_W
    row = jnp.repeat(jnp.arange(rows), GRID_W).astype(jnp.float32)
    col = jnp.tile(jnp.arange(GRID_W), rows).astype(jnp.float32)
    n_freq = HEAD_DIM // 4
    inv_freq = ROPE_THETA ** (-jnp.arange(n_freq, dtype=jnp.float32) / n_freq)
    ang = jnp.concatenate([row[:, None] * inv_freq, col[:, None] * inv_freq], axis=-1)
    return jnp.cos(ang).astype(dtype), jnp.sin(ang).astype(dtype)


def apply_rope(x, cos, sin):
    half = HEAD_DIM // 2
    shape = (1, x.shape[1]) + (1,) * (x.ndim - 3) + (half,)
    cos = cos.reshape(shape)
    sin = sin.reshape(shape)
    x1, x2 = x[..., :half], x[..., half:]
    return jnp.concatenate([x1 * cos - x2 * sin, x1 * sin + x2 * cos], axis=-1)


def qkv_proj(h, w_qkv):
    b, t, _ = h.shape
    qkv = h @ w_qkv
    q, k, v = jnp.split(qkv, [N_HEADS * HEAD_DIM, (N_HEADS + N_KV_HEADS) * HEAD_DIM], axis=-1)
    return (q.reshape(b, t, N_KV_HEADS, GQA_GROUP, HEAD_DIM),
            k.reshape(b, t, N_KV_HEADS, HEAD_DIM),
            v.reshape(b, t, N_KV_HEADS, HEAD_DIM))


def windowed_attention(h_lat, h_ctx, w_qkv, w_o, sink, cos, sin, need_ctx_out):
    b, s, _ = h_lat.shape
    l = h_ctx.shape[1]
    nb = s // BLOCK
    scale = HEAD_DIM ** -0.5
    q, k, v = qkv_proj(h_lat, w_qkv)
    qc, kc, vc = qkv_proj(h_ctx, w_qkv)
    q = apply_rope(q, cos, sin)
    k = apply_rope(k, cos, sin)
    sink_f = sink.astype(jnp.float32).reshape(N_KV_HEADS, GQA_GROUP)

    qb = q.reshape(b, nb, BLOCK, N_KV_HEADS, GQA_GROUP, HEAD_DIM)
    pad = ((0, 0), (BLOCK, BLOCK), (0, 0), (0, 0))
    kp = jnp.pad(k, pad).reshape(b, nb + 2, BLOCK, N_KV_HEADS, HEAD_DIM)
    vp = jnp.pad(v, pad).reshape(b, nb + 2, BLOCK, N_KV_HEADS, HEAD_DIM)
    kb = jnp.concatenate([kp[:, :nb], kp[:, 1:nb + 1], kp[:, 2:]], axis=2)
    vb = jnp.concatenate([vp[:, :nb], vp[:, 1:nb + 1], vp[:, 2:]], axis=2)
    s_band = jnp.einsum('bnqkgd,bnjkd->bnkgqj', qb, kb).astype(jnp.float32) * scale
    q_pos = jnp.arange(nb)[:, None] * BLOCK + jnp.arange(BLOCK)[None, :]
    k_pos = jnp.arange(nb)[:, None] * BLOCK - BLOCK + jnp.arange(3 * BLOCK)[None, :]
    valid = ((jnp.abs(q_pos[:, :, None] - k_pos[:, None, :]) <= WINDOW)
             & (k_pos[:, None, :] >= 0) & (k_pos[:, None, :] < s))
    s_band = jnp.where(valid[None, :, None, None], s_band, NEG)
    s_ctx = jnp.einsum('bnqkgd,blkd->bnkgql', qb, kc).astype(jnp.float32) * scale
    sink_col = jnp.broadcast_to(sink_f[None, None, :, :, None, None], s_band.shape[:-1] + (1,))
    p = jax.nn.softmax(jnp.concatenate([s_band, s_ctx, sink_col], axis=-1), axis=-1)
    p_band = p[..., :3 * BLOCK].astype(v.dtype)
    p_ctx = p[..., 3 * BLOCK:3 * BLOCK + l].astype(v.dtype)
    o = (jnp.einsum('bnkgqj,bnjkd->bnqkgd', p_band, vb)
         + jnp.einsum('bnkgql,blkd->bnqkgd', p_ctx, vc))
    o_lat = o.reshape(b, s, D_MODEL) @ w_o

    o_ctx = None
    if need_ctx_out:
        s_cc = jnp.einsum('blkgd,bmkd->bkglm', qc, kc).astype(jnp.float32) * scale
        sink_cc = jnp.broadcast_to(sink_f[None, :, :, None, None], s_cc.shape[:-1] + (1,))
        pc = jax.nn.softmax(jnp.concatenate([s_cc, sink_cc], axis=-1), axis=-1)[..., :l].astype(vc.dtype)
        oc = jnp.einsum('bkglm,bmkd->blkgd', pc, vc)
        o_ctx = oc.reshape(b, l, D_MODEL) @ w_o
    return o_lat, o_ctx


def short_conv_mixer(h, w_in, conv_w, w_out):
    bcx = h @ w_in
    b_gate, c_gate, xin = jnp.split(bcx, 3, axis=-1)
    u = c_gate * xin
    u = lax.conv_general_dilated(u, conv_w[:, None, :].astype(u.dtype), window_strides=(1,),
                                 padding=((CONV_WIDTH // 2, CONV_WIDTH // 2),),
                                 dimension_numbers=('NWC', 'WIO', 'NWC'),
                                 feature_group_count=D_MODEL)
    return (b_gate * u) @ w_out


def grouped_route(hf, w_router, router_bias):
    n = hf.shape[0]
    scores = jax.nn.sigmoid((hf @ w_router).astype(jnp.float32))
    sel = scores + router_bias.astype(jnp.float32)
    grp_score = lax.top_k(sel.reshape(n, N_GROUPS, EXPERTS_PER_GROUP), 2)[0].sum(-1)
    g_idx = jnp.argmax(grp_score, axis=-1)
    expert_mask = jnp.repeat(jax.nn.one_hot(g_idx, N_GROUPS, dtype=jnp.bool_), EXPERTS_PER_GROUP, axis=-1)
    _, top_idx = lax.top_k(jnp.where(expert_mask, sel, NEG), TOP_K)
    top_w = jnp.take_along_axis(scores, top_idx, axis=-1)
    top_w = top_w / jnp.sum(top_w, axis=-1, keepdims=True)
    return jnp.sum(jax.nn.one_hot(top_idx, N_EXPERTS, dtype=jnp.float32) * top_w[..., None], axis=1)


def moe(h, w_router, router_bias, w_gate_up, w_down):
    shp = h.shape
    hf = h.reshape(-1, D_MODEL)
    combine = grouped_route(hf, w_router, router_bias).astype(hf.dtype)
    out = jnp.zeros_like(hf)
    for e in range(N_EXPERTS):
        gu = hf @ w_gate_up[e]
        a = jax.nn.silu(gu[:, :D_FF_EXPERT]) * gu[:, D_FF_EXPERT:]
        out = out + combine[:, e:e + 1] * (a @ w_down[e])
    return out.reshape(shp)


def setup_inputs(seed: int = 0) -> dict:
    key = jax.random.key(seed)
    ks = jax.random.split(key, 20)
    f32 = jnp.float32
    nrm = lambda k, shape, s: jax.random.normal(k, shape, f32) * s
    D = D_MODEL
    return {
        'x': nrm(ks[0], (BATCH, SEQ, D), 1.0),
        'c': nrm(ks[1], (BATCH, D), 1.0),
        'ctx': nrm(ks[2], (BATCH, CTX_LEN, D), 1.0),
        'c_ctx': nrm(ks[3], (D,), 1.0),
        'w_ada': nrm(ks[4], (DEPTH, D, N_MOD * D), 0.5 * D ** -0.5),
        'b_ada': nrm(ks[5], (DEPTH, N_MOD * D), 0.02),
        'norm1_g': 1.0 + nrm(ks[6], (DEPTH, D), 0.02),
        'norm2_g': 1.0 + nrm(ks[7], (DEPTH, D), 0.02),
        'attn_w_qkv': nrm(ks[8], (N_ATTN_LAYERS, D, QKV_DIM), D ** -0.5),
        'attn_w_o': nrm(ks[9], (N_ATTN_LAYERS, D, D), D ** -0.5),
        'attn_sink': nrm(ks[10], (N_ATTN_LAYERS, N_HEADS), 0.5),
        'conv_w_in': nrm(ks[11], (N_CONV_LAYERS, D, 3 * D), D ** -0.5),
        'conv_w': nrm(ks[12], (N_CONV_LAYERS, CONV_WIDTH, D), CONV_WIDTH ** -0.5),
        'conv_w_out': nrm(ks[13], (N_CONV_LAYERS, D, D), D ** -0.5),
        'w_router': nrm(ks[14], (D, N_EXPERTS), D ** -0.5),
        'router_bias': nrm(ks[15], (N_EXPERTS,), 0.01),
        'moe_w_gate_up': nrm(ks[16], (DEPTH, N_EXPERTS, D, 2 * D_FF_EXPERT), D ** -0.5),
        'moe_w_down': nrm(ks[17], (DEPTH, N_EXPERTS, D_FF_EXPERT, D), D_FF_EXPERT ** -0.5),
        'final_g': 1.0 + nrm(ks[18], (D,), 0.02),
    }


def reference(x, c, ctx, c_ctx, w_ada, b_ada, norm1_g, norm2_g, attn_w_qkv, attn_w_o, attn_sink,
              conv_w_in, conv_w, conv_w_out, w_router, router_bias, moe_w_gate_up, moe_w_down, final_g):
    b, s, _ = x.shape
    cos, sin = rope_2d(s, x.dtype)
    silu_c = jax.nn.silu(c)
    silu_cc = jax.nn.silu(c_ctx)
    h_lat, h_ctx = x, ctx
    for i in range(DEPTH):
        last = i == DEPTH - 1
        j = i // N_MIXERS
        m_lat = (silu_c @ w_ada[i] + b_ada[i]).reshape(b, 1, N_MOD, D_MODEL)
        m_ctx = (silu_cc @ w_ada[i] + b_ada[i]).reshape(1, 1, N_MOD, D_MODEL)
        sh1, sc1, g1, sh2, sc2, g2 = [m_lat[:, :, n] for n in range(N_MOD)]
        csh1, csc1, cg1, csh2, csc2, cg2 = [m_ctx[:, :, n] for n in range(N_MOD)]

        a_lat = modulate(rmsnorm(h_lat, norm1_g[i]), sh1, sc1)
        a_ctx = modulate(rmsnorm(h_ctx, norm1_g[i]), csh1, csc1)
        if i % N_MIXERS == 0:
            o_lat, o_ctx = windowed_attention(a_lat, a_ctx, attn_w_qkv[j], attn_w_o[j], attn_sink[j],
                                              cos, sin, not last)
        else:
            o_lat = short_conv_mixer(a_lat, conv_w_in[j], conv_w[j], conv_w_out[j])
            o_ctx = None if last else short_conv_mixer(a_ctx, conv_w_in[j], conv_w[j], conv_w_out[j])
        h_lat = h_lat + g1 * o_lat
        if not last:
            h_ctx = h_ctx + cg1 * o_ctx

        f_lat = modulate(rmsnorm(h_lat, norm2_g[i]), sh2, sc2)
        if last:
            h_lat = h_lat + g2 * moe(f_lat, w_router, router_bias, moe_w_gate_up[i], moe_w_down[i])
        else:
            f_ctx = modulate(rmsnorm(h_ctx, norm2_g[i]), csh2, csc2)
            f = moe(jnp.concatenate([f_lat, f_ctx], axis=1), w_router, router_bias,
                    moe_w_gate_up[i], moe_w_down[i])
            h_lat = h_lat + g2 * f[:, :s]
            h_ctx = h_ctx + cg2 * f[:, s:]
    return rmsnorm(h_lat, final_g)
```

```python
import functools

import jax
import jax.numpy as jnp
from jax import lax
from jax.experimental import pallas as pl
from jax.experimental.pallas import tpu as pltpu

D = 4096
N_HEADS = 32
N_KV = 8
GQA = 4
DH = 128
QKV = (N_HEADS + 2 * N_KV) * DH
GRID_W = 64
WINDOW = 128
ROPE_THETA = 10000.0
N_EXP = 16
N_GRP = 4
EXP_PER_GRP = 4
D_FF = 1024
N_MOD = 6
EPS = 1e-6
NEG = -1e30

LANE = 128
SUBLANE = 8
VMEM_LIMIT = 56 * 1024 * 1024

TM_MM = 512
TN_MM = 1024
TN_CONV = 256
TM_ROW = 512
TM_CONV = 256
TM_EXP = 256
TF_EXP = 512
TC_COMB = 256

F32 = jnp.float32
BF16 = jnp.bfloat16


def _params(n_axes):
    return pltpu.CompilerParams(dimension_semantics=("arbitrary",) * n_axes,
                                vmem_limit_bytes=VMEM_LIMIT)


def _dot(a, b):
    return lax.dot_general(a, b, (((1,), (0,)), ((), ())), preferred_element_type=F32)


def _dot_nt(a, b):
    return lax.dot_general(a, b, (((1,), (1,)), ((), ())), preferred_element_type=F32)


def _ada_kernel(c_ref, w_ref, b_ref, o_ref):
    s = c_ref[...]
    s = (s * jax.nn.sigmoid(s)).astype(BF16)
    o_ref[...] = _dot(s, w_ref[...]) + b_ref[...]


def _ada(c8, w_ada, b_ada):
    depth = w_ada.shape[0]
    n = w_ada.shape[2]
    return pl.pallas_call(
        _ada_kernel,
        grid=(depth, n // TN_MM),
        in_specs=[
            pl.BlockSpec((SUBLANE, D), lambda l, j: (0, 0)),
            pl.BlockSpec((None, D, TN_MM), lambda l, j: (l, 0, j)),
            pl.BlockSpec((None, 1, TN_MM), lambda l, j: (l, 0, j)),
        ],
        out_specs=pl.BlockSpec((None, SUBLANE, TN_MM), lambda l, j: (l, 0, j)),
        out_shape=jax.ShapeDtypeStruct((depth, SUBLANE, n), F32),
        compiler_params=_params(2),
        name="ada_mod",
    )(c8, w_ada, b_ada.reshape(depth, 1, n))


def _mod_row(layer, cond_row, which):
    return (layer * SUBLANE + cond_row) * N_MOD + which


def _rms(x, g):
    return x * lax.rsqrt(jnp.mean(x * x, axis=-1, keepdims=True) + EPS) * g


def _normmod_kernel(h_ref, g_ref, sh_ref, sc_ref, o_ref):
    y = _rms(h_ref[...], g_ref[...])
    o_ref[...] = (y * (1.0 + sc_ref[...]) + sh_ref[...]).astype(o_ref.dtype)


def _normmod(h, g, mod3, layer, which_shift, rows_per_cond, cond_base):
    m = h.shape[0]
    tm = min(TM_ROW, rows_per_cond)

    def row(which):
        return lambda i: (_mod_row(layer, cond_base + (i * tm) // rows_per_cond, which), 0, 0)

    return pl.pallas_call(
        _normmod_kernel,
        grid=(m // tm,),
        in_specs=[
            pl.BlockSpec((tm, D), lambda i: (i, 0)),
            pl.BlockSpec((1, D), lambda i: (0, 0)),
            pl.BlockSpec((None, 1, D), row(which_shift)),
            pl.BlockSpec((None, 1, D), row(which_shift + 1)),
        ],
        out_specs=pl.BlockSpec((tm, D), lambda i: (i, 0)),
        out_shape=jax.ShapeDtypeStruct((m, D), BF16),
        compiler_params=_params(1),
        name="norm_modulate",
    )(h, g.reshape(1, D), mod3, mod3)


def _mm_plain_kernel(x_ref, w_ref, o_ref):
    o_ref[...] = _dot(x_ref[...], w_ref[...]).astype(o_ref.dtype)


def _mm_rope_kernel(x_ref, w_ref, cos_ref, sin_ref, o_ref, *, n_rope_tiles):
    acc = _dot(x_ref[...], w_ref[...])

    @pl.when(pl.program_id(0) < n_rope_tiles)
    def _():
        cosf = cos_ref[...]
        sinf = sin_ref[...]
        for hh in range(acc.shape[1] // DH):
            xh = acc[:, hh * DH:(hh + 1) * DH]
            o_ref[:, hh * DH:(hh + 1) * DH] = (
                xh * cosf + pltpu.roll(xh, DH // 2, axis=1) * sinf).astype(o_ref.dtype)

    @pl.when(pl.program_id(0) >= n_rope_tiles)
    def _():
        o_ref[...] = acc.astype(o_ref.dtype)


def _mm_resid_kernel(x_ref, w_ref, h_ref, gate_ref, o_ref):
    o_ref[...] = h_ref[...] + gate_ref[...] * _dot(x_ref[...], w_ref[...])


def _qkv_proj(a, w_qkv, rope=None):
    m = a.shape[0]
    grid = (QKV // TN_MM, m // TM_MM)
    x_spec = pl.BlockSpec((TM_MM, D), lambda j, i: (i, 0))
    w_spec = pl.BlockSpec((D, TN_MM), lambda j, i: (0, j))
    o_spec = pl.BlockSpec((TM_MM, TN_MM), lambda j, i: (i, j))
    out_shape = jax.ShapeDtypeStruct((m, QKV), BF16)
    if rope is None:
        return pl.pallas_call(_mm_plain_kernel, grid=grid, in_specs=[x_spec, w_spec], out_specs=o_spec,
                              out_shape=out_shape, compiler_params=_params(2), name="qkv_ctx")(a, w_qkv)
    cosf, sinf = rope
    seq_tiles = cosf.shape[0] // TM_MM
    t_spec = pl.BlockSpec((TM_MM, DH), lambda j, i: (i % seq_tiles, 0))
    n_rope_tiles = (N_HEADS + N_KV) * DH // TN_MM
    return pl.pallas_call(
        functools.partial(_mm_rope_kernel, n_rope_tiles=n_rope_tiles),
        grid=grid, in_specs=[x_spec, w_spec, t_spec, t_spec], out_specs=o_spec,
        out_shape=out_shape, compiler_params=_params(2), name="qkv_rope")(a, w_qkv, cosf, sinf)


def _proj_resid(x, w, h, mod3, layer, which_gate, rows_per_cond, cond_base):
    m = x.shape[0]
    tm = min(TM_MM, rows_per_cond)

    def gate_map(j, i):
        return (_mod_row(layer, cond_base + (i * tm) // rows_per_cond, which_gate), 0, j)

    return pl.pallas_call(
        _mm_resid_kernel,
        grid=(D // TN_MM, m // tm),
        in_specs=[
            pl.BlockSpec((tm, D), lambda j, i: (i, 0)),
            pl.BlockSpec((D, TN_MM), lambda j, i: (0, j)),
            pl.BlockSpec((tm, TN_MM), lambda j, i: (i, j)),
            pl.BlockSpec((None, 1, TN_MM), gate_map),
        ],
        out_specs=pl.BlockSpec((tm, TN_MM), lambda j, i: (i, j)),
        out_shape=jax.ShapeDtypeStruct((m, D), F32),
        compiler_params=_params(2),
        name="proj_residual",
    )(x, w, h, mod3)


def _attn_kernel(sink_ref, q_ref, k_ref, v_ref, kc_ref, vc_ref, o_ref, *, band, seq):
    kh = pl.program_id(1)
    scale = DH ** -0.5
    kc = kc_ref[...]
    vc = vc_ref[...]
    n_ctx = kc.shape[0]
    rows = GQA * WINDOW
    row = lax.broadcasted_iota(jnp.int32, (rows, 1), 0)
    sink_col = jnp.full((rows, 1), sink_ref[kh * GQA + GQA - 1], F32)
    for g in range(GQA - 2, -1, -1):
        sink_col = jnp.where(row < (g + 1) * WINDOW, sink_ref[kh * GQA + g], sink_col)
    n_band = 3 * WINDOW if band else 0

    def block(n, carry):
        q0 = pl.multiple_of(n * WINDOW, WINDOW)
        qb = q_ref[pl.ds(q0, WINDOW), :]
        q4 = jnp.concatenate([qb[:, g * DH:(g + 1) * DH] for g in range(GQA)], axis=0)
        if band:
            ws = pl.multiple_of(jnp.clip((n - 1) * WINDOW, 0, seq - n_band), WINDOW)
            keys = jnp.concatenate([k_ref[pl.ds(ws, n_band), :], kc], axis=0)
            vals = jnp.concatenate([v_ref[pl.ds(ws, n_band), :], vc], axis=0)
        else:
            keys, vals = kc, vc
        s = _dot_nt(q4, keys) * scale
        if band:
            q_pos = q0 + (row & (WINDOW - 1))
            col = lax.broadcasted_iota(jnp.int32, (1, n_band + n_ctx), 1)
            valid = (jnp.abs(q_pos - (ws + col)) <= WINDOW) | (col >= n_band)
            s = jnp.where(valid, s, NEG)
        m = jnp.maximum(jnp.max(s, axis=-1, keepdims=True), sink_col)
        p = jnp.exp(s - m)
        denom = jnp.sum(p, axis=-1, keepdims=True) + jnp.exp(sink_col - m)
        o = _dot(p.astype(BF16), vals) / denom
        o_ref[pl.ds(q0, WINDOW), :] = jnp.concatenate(
            [o[g * WINDOW:(g + 1) * WINDOW] for g in range(GQA)], axis=1).astype(o_ref.dtype)
        return carry

    lax.fori_loop(0, q_ref.shape[0] // WINDOW, block, 0)


def _attention(qkv_q, qkv_kv, qkv_ctx, sink, batch, band):
    seq_q = qkv_q.shape[0] // batch
    seq_k = qkv_kv.shape[0] // batch
    n_ctx = qkv_ctx.shape[0] // batch
    k_col = N_HEADS
    v_col = N_HEADS + N_KV
    return pl.pallas_call(
        functools.partial(_attn_kernel, band=band, seq=seq_k),
        grid_spec=pltpu.PrefetchScalarGridSpec(
            num_scalar_prefetch=1,
            grid=(batch, N_KV),
            in_specs=[
                pl.BlockSpec((seq_q, GQA * DH), lambda b, k, s: (b, k)),
                pl.BlockSpec((seq_k, DH), lambda b, k, s: (b, k_col + k)),
                pl.BlockSpec((seq_k, DH), lambda b, k, s: (b, v_col + k)),
                pl.BlockSpec((n_ctx, DH), lambda b, k, s: (b, k_col + k)),
                pl.BlockSpec((n_ctx, DH), lambda b, k, s: (b, v_col + k)),
            ],
            out_specs=pl.BlockSpec((seq_q, GQA * DH), lambda b, k, s: (b, k)),
        ),
        out_shape=jax.ShapeDtypeStruct((qkv_q.shape[0], D), BF16),
        compiler_params=_params(2),
        name="window_attn" if band else "ctx_attn",
    )(sink, qkv_q, qkv_kv, qkv_kv, qkv_ctx, qkv_ctx)


def _conv_in_kernel(x_ref, wb_ref, wc_ref, wx_ref, b_ref, u_ref):
    x = x_ref[...]
    b_ref[...] = _dot(x, wb_ref[...])
    u_ref[...] = _dot(x, wc_ref[...]) * _dot(x, wx_ref[...])


def _conv_in(a, w_in):
    m = a.shape[0]
    nt = D // TN_CONV
    o_spec = pl.BlockSpec((TM_MM, TN_CONV), lambda j, i: (i, j))
    return pl.pallas_call(
        _conv_in_kernel,
        grid=(nt, m // TM_MM),
        in_specs=[
            pl.BlockSpec((TM_MM, D), lambda j, i: (i, 0)),
            pl.BlockSpec((D, TN_CONV), lambda j, i: (0, j)),
            pl.BlockSpec((D, TN_CONV), lambda j, i: (0, nt + j)),
            pl.BlockSpec((D, TN_CONV), lambda j, i: (0, 2 * nt + j)),
        ],
        out_specs=[o_spec, o_spec],
        out_shape=[jax.ShapeDtypeStruct((m, D), F32)] * 2,
        compiler_params=_params(2),
        name="conv_in_proj",
    )(a, w_in, w_in, w_in)


def _conv_gate_kernel(b_ref, u_ref, up_ref, un_ref, w_ref, o_ref, *, seq):
    i = pl.program_id(0)
    tm = u_ref.shape[0]
    u = u_ref[...]
    t0 = i * tm
    prev_row = jnp.where(lax.rem(t0, seq) == 0, 0.0, up_ref[SUBLANE - 1:SUBLANE, :])
    next_row = jnp.where(lax.rem(t0 + tm, seq) == 0, 0.0, un_ref[0:1, :])
    row = lax.broadcasted_iota(jnp.int32, u.shape, 0)
    u_prev = jnp.where(row == 0, prev_row, pltpu.roll(u, 1, axis=0))
    u_next = jnp.where(row == tm - 1, next_row, pltpu.roll(u, tm - 1, axis=0))
    y = u_prev * w_ref[0:1, :] + u * w_ref[1:2, :] + u_next * w_ref[2:3, :]
    o_ref[...] = (b_ref[...] * y).astype(o_ref.dtype)


def _conv_gate(b, u, conv_w, seq):
    m = u.shape[0]
    tm = TM_CONV
    per = tm // SUBLANE
    last = m // SUBLANE - 1
    return pl.pallas_call(
        functools.partial(_conv_gate_kernel, seq=seq),
        grid=(m // tm,),
        in_specs=[
            pl.BlockSpec((tm, D), lambda i: (i, 0)),
            pl.BlockSpec((tm, D), lambda i: (i, 0)),
            pl.BlockSpec((SUBLANE, D), lambda i: (jnp.maximum(i * per - 1, 0), 0)),
            pl.BlockSpec((SUBLANE, D), lambda i: (jnp.minimum((i + 1) * per, last), 0)),
            pl.BlockSpec((3, D), lambda i: (0, 0)),
        ],
        out_specs=pl.BlockSpec((tm, D), lambda i: (i, 0)),
        out_shape=jax.ShapeDtypeStruct((m, D), BF16),
        compiler_params=_params(1),
        name="conv_gate",
    )(b, u, u, u, conv_w)


def _route_kernel(h_ref, g_ref, sh_ref, sc_ref, wr_ref, bias_ref, f_ref, route_ref, cnt_ref, run_ref):
    i = pl.program_id(0)
    tm = h_ref.shape[0]

    @pl.when(i == 0)
    def _():
        run_ref[...] = jnp.zeros_like(run_ref)

    f = _rms(h_ref[...], g_ref[...]) * (1.0 + sc_ref[...]) + sh_ref[...]
    f_ref[...] = f
    logits = jnp.dot(f, wr_ref[...], preferred_element_type=F32, precision=lax.Precision.HIGHEST)
    lt = logits.T[:N_EXP]
    score = jax.nn.sigmoid(lt)
    sel = score + bias_ref[...]
    sel_rows = [sel[e:e + 1] for e in range(N_EXP)]
    score_rows = [score[e:e + 1] for e in range(N_EXP)]

    grp_scores = []
    for gi in range(N_GRP):
        a, b, c, d = sel_rows[gi * EXP_PER_GRP:(gi + 1) * EXP_PER_GRP]
        hi1, lo1 = jnp.maximum(a, b), jnp.minimum(a, b)
        hi2, lo2 = jnp.maximum(c, d), jnp.minimum(c, d)
        grp_scores.append(jnp.maximum(hi1, hi2) + jnp.maximum(jnp.minimum(hi1, hi2), jnp.maximum(lo1, lo2)))
    best = grp_scores[0]
    g_idx = jnp.zeros_like(best, dtype=jnp.int32)
    for gi in range(1, N_GRP):
        better = grp_scores[gi] > best
        g_idx = jnp.where(better, gi, g_idx)
        best = jnp.where(better, grp_scores[gi], best)

    masked = [jnp.where(g_idx == e // EXP_PER_GRP, sel_rows[e], NEG) for e in range(N_EXP)]
    v1 = masked[0]
    e1 = jnp.zeros_like(g_idx)
    for e in range(1, N_EXP):
        better = masked[e] > v1
        e1 = jnp.where(better, e, e1)
        v1 = jnp.where(better, masked[e], v1)
    v2 = jnp.full_like(v1, -jnp.inf)
    e2 = jnp.zeros_like(g_idx)
    for e in range(N_EXP):
        better = (masked[e] > v2) & (e1 != e)
        e2 = jnp.where(better, e, e2)
        v2 = jnp.where(better, masked[e], v2)
    s1 = jnp.zeros_like(v1)
    s2 = jnp.zeros_like(v1)
    for e in range(N_EXP):
        s1 = jnp.where(e1 == e, score_rows[e], s1)
        s2 = jnp.where(e2 == e, score_rows[e], s2)
    tot = s1 + s2
    w1 = s1 / tot
    w2 = s2 / tot

    e_iota = lax.broadcasted_iota(jnp.int32, (N_EXP, tm), 0)
    hit = (e_iota == e1) | (e_iota == e2)
    onehot = jnp.where(hit, 1.0, 0.0).astype(BF16)
    r_i = lax.broadcasted_iota(jnp.int32, (tm, tm), 0)
    c_i = lax.broadcasted_iota(jnp.int32, (tm, tm), 1)
    upper = jnp.where(r_i <= c_i, 1.0, 0.0).astype(BF16)
    prefix = _dot(onehot, upper)
    rank = prefix - 1.0 + run_ref[:, 0:1]
    rank1 = jnp.sum(jnp.where(e_iota == e1, rank, 0.0), axis=0, keepdims=True)
    rank2 = jnp.sum(jnp.where(e_iota == e2, rank, 0.0), axis=0, keepdims=True)
    run_ref[...] = run_ref[...] + jnp.sum(jnp.where(hit, 1.0, 0.0), axis=1, keepdims=True)
    cnt_ref[...] = run_ref[...]

    route_ref[0:1, :] = e1.astype(F32)
    route_ref[1:2, :] = e2.astype(F32)
    route_ref[2:3, :] = w1
    route_ref[3:4, :] = w2
    route_ref[4:5, :] = rank1
    route_ref[5:6, :] = rank2
    route_ref[6:8, :] = jnp.zeros((2, tm), F32)


def _route(h, g, mod3, layer, rows_per_cond, cond_base, wr_pad, bias_col):
    m = h.shape[0]
    tm = min(TM_ROW, rows_per_cond)

    def row(which):
        return lambda i: (_mod_row(layer, cond_base + (i * tm) // rows_per_cond, which), 0, 0)

    return pl.pallas_call(
        _route_kernel,
        grid=(m // tm,),
        in_specs=[
            pl.BlockSpec((tm, D), lambda i: (i, 0)),
            pl.BlockSpec((1, D), lambda i: (0, 0)),
            pl.BlockSpec((None, 1, D), row(3)),
            pl.BlockSpec((None, 1, D), row(4)),
            pl.BlockSpec((D, LANE), lambda i: (0, 0)),
            pl.BlockSpec((N_EXP, 1), lambda i: (0, 0)),
        ],
        out_specs=[
            pl.BlockSpec((tm, D), lambda i: (i, 0)),
            pl.BlockSpec((SUBLANE, tm), lambda i: (0, i)),
            pl.BlockSpec((N_EXP, LANE), lambda i: (0, 0)),
        ],
        out_shape=[
            jax.ShapeDtypeStruct((m, D), F32),
            jax.ShapeDtypeStruct((SUBLANE, m), F32),
            jax.ShapeDtypeStruct((N_EXP, LANE), F32),
        ],
        scratch_shapes=[pltpu.VMEM((N_EXP, LANE), F32)],
        compiler_params=_params(1),
        name="norm_route",
    )(h, g.reshape(1, D), mod3, mod3, wr_pad, bias_col)


def _row_gather(idx_ref, src_hbm, buf, sem, slot, n_rows):
    def issue(k, carry):
        pltpu.make_async_copy(src_hbm.at[pl.ds(idx_ref[0, k], 1)], buf.at[slot, pl.ds(k, 1)], sem.at[slot]).start()
        return carry
    lax.fori_loop(0, n_rows, issue, 0)


def _row_gather_wait(src_hbm, buf, sem, slot, n_rows):
    pltpu.make_async_copy(src_hbm.at[pl.ds(0, n_rows)], buf.at[slot], sem.at[slot]).wait()


def _dispatch_kernel(nvalid_ref, idx_ref, idx_next_ref, f_hbm, o_ref, buf, sem):
    i = pl.program_id(0)
    tm = o_ref.shape[0]
    n_valid = nvalid_ref[0]
    slot = lax.rem(i, 2)

    @pl.when((i == 0) & (n_valid > 0))
    def _():
        _row_gather(idx_ref, f_hbm, buf, sem, 0, tm)

    @pl.when(i + 1 < n_valid)
    def _():
        _row_gather(idx_next_ref, f_hbm, buf, sem, 1 - slot, tm)

    @pl.when(i < n_valid)
    def _():
        _row_gather_wait(f_hbm, buf, sem, slot, tm)
        o_ref[...] = buf[slot].astype(o_ref.dtype)

    @pl.when(i >= n_valid)
    def _():
        o_ref[...] = jnp.zeros_like(o_ref)


def _dispatch(f, src_tiles, n_valid_tiles):
    nt = src_tiles.shape[0]
    idx_spec = pl.BlockSpec((None, 1, TM_EXP), lambda i, nv: (i, 0, 0), memory_space=pltpu.SMEM)
    idx_next_spec = pl.BlockSpec((None, 1, TM_EXP), lambda i, nv: (jnp.minimum(i + 1, nt - 1), 0, 0),
                                 memory_space=pltpu.SMEM)
    return pl.pallas_call(
        _dispatch_kernel,
        grid_spec=pltpu.PrefetchScalarGridSpec(
            num_scalar_prefetch=1,
            grid=(nt,),
            in_specs=[idx_spec, idx_next_spec, pl.BlockSpec(memory_space=pl.ANY)],
            out_specs=pl.BlockSpec((TM_EXP, D), lambda i, nv: (i, 0)),
            scratch_shapes=[pltpu.VMEM((2, TM_EXP, D), F32), pltpu.SemaphoreType.DMA((2,))],
        ),
        out_shape=jax.ShapeDtypeStruct((nt * TM_EXP, D), BF16),
        compiler_params=_params(1),
        name="moe_dispatch",
    )(n_valid_tiles, src_tiles, src_tiles, f)


ST_EXPERT, ST_WCOL, ST_XROW, ST_OROW, ST_OCOL, ST_VALID = range(6)


def _gate_up_kernel(st_ref, x_ref, wg_ref, wu_ref, o_ref):
    s = pl.program_id(0)

    @pl.when(st_ref[ST_VALID, s] > 0)
    def _():
        x = x_ref[...]
        gate = _dot(x, wg_ref[...])
        up = _dot(x, wu_ref[...])
        o_ref[...] = (gate * jax.nn.sigmoid(gate) * up).astype(o_ref.dtype)

    @pl.when(st_ref[ST_VALID, s] == 0)
    def _():
        o_ref[...] = jnp.zeros_like(o_ref)


def _gate_up(xs, w_gate_up, layer, steps):
    n_steps = steps.shape[1]
    nf = D_FF // TF_EXP
    return pl.pallas_call(
        _gate_up_kernel,
        grid_spec=pltpu.PrefetchScalarGridSpec(
            num_scalar_prefetch=1,
            grid=(n_steps,),
            in_specs=[
                pl.BlockSpec((TM_EXP, D), lambda s, st: (st[ST_XROW, s], 0)),
                pl.BlockSpec((None, None, D, TF_EXP), lambda s, st: (layer, st[ST_EXPERT, s], 0, st[ST_WCOL, s])),
                pl.BlockSpec((None, None, D, TF_EXP),
                             lambda s, st: (layer, st[ST_EXPERT, s], 0, nf + st[ST_WCOL, s])),
            ],
            out_specs=pl.BlockSpec((TM_EXP, TF_EXP), lambda s, st: (st[ST_OROW, s], st[ST_OCOL, s])),
        ),
        out_shape=jax.ShapeDtypeStruct((xs.shape[0], D_FF), BF16),
        compiler_params=_params(1),
        name="moe_gate_up",
    )(steps, xs, w_gate_up, w_gate_up)


def _down_kernel(te_ref, ts_ref, a_ref, w_ref, o_ref):
    i = pl.program_id(0)

    @pl.when(ts_ref[i] == i)
    def _():
        o_ref[...] = _dot(a_ref[...], w_ref[...])

    @pl.when(ts_ref[i] != i)
    def _():
        o_ref[...] = jnp.zeros_like(o_ref)


def _down(act, w_down, layer, tile_e, tile_src):
    nt = tile_e.shape[0]
    return pl.pallas_call(
        _down_kernel,
        grid_spec=pltpu.PrefetchScalarGridSpec(
            num_scalar_prefetch=2,
            grid=(nt,),
            in_specs=[
                pl.BlockSpec((TM_EXP, D_FF), lambda i, te, ts: (ts[i], 0)),
                pl.BlockSpec((None, None, D_FF, D), lambda i, te, ts: (layer, te[i], 0, 0)),
            ],
            out_specs=pl.BlockSpec((TM_EXP, D), lambda i, te, ts: (i, 0)),
        ),
        out_shape=jax.ShapeDtypeStruct((act.shape[0], D), F32),
        compiler_params=_params(1),
        name="moe_down",
    )(tile_e, tile_src, act, w_down)


def _combine_kernel(d1_ref, d2_ref, d1n_ref, d2n_ref, y_hbm, h_ref, w1_ref, w2_ref, gate_ref, fg_ref, o_ref,
                    buf1, buf2, sem1, sem2, *, final_norm):
    i = pl.program_id(0)
    n = pl.num_programs(0)
    tc = o_ref.shape[0]
    slot = lax.rem(i, 2)

    @pl.when(i == 0)
    def _():
        _row_gather(d1_ref, y_hbm, buf1, sem1, 0, tc)
        _row_gather(d2_ref, y_hbm, buf2, sem2, 0, tc)

    @pl.when(i + 1 < n)
    def _():
        _row_gather(d1n_ref, y_hbm, buf1, sem1, 1 - slot, tc)
        _row_gather(d2n_ref, y_hbm, buf2, sem2, 1 - slot, tc)

    _row_gather_wait(y_hbm, buf1, sem1, slot, tc)
    _row_gather_wait(y_hbm, buf2, sem2, slot, tc)
    moe = w1_ref[:, 0:1] * buf1[slot] + w2_ref[:, 0:1] * buf2[slot]
    out = h_ref[...] + gate_ref[...] * moe
    if final_norm:
        out = _rms(out, fg_ref[...])
    o_ref[...] = out


def _combine(y, d1_tiles, d2_tiles, h, w1b, w2b, mod3, layer, rows_per_cond, cond_base, final_g):
    m = h.shape[0]
    tc = TC_COMB
    nt = m // tc
    cur = pl.BlockSpec((None, 1, tc), lambda i: (i, 0, 0), memory_space=pltpu.SMEM)
    nxt = pl.BlockSpec((None, 1, tc), lambda i: (jnp.minimum(i + 1, nt - 1), 0, 0), memory_space=pltpu.SMEM)
    fg = jnp.ones((1, D), F32) if final_g is None else final_g.reshape(1, D)
    return pl.pallas_call(
        functools.partial(_combine_kernel, final_norm=final_g is not None),
        grid=(nt,),
        in_specs=[
            cur, cur, nxt, nxt,
            pl.BlockSpec(memory_space=pl.ANY),
            pl.BlockSpec((tc, D), lambda i: (i, 0)),
            pl.BlockSpec((tc, LANE), lambda i: (i, 0)),
            pl.BlockSpec((tc, LANE), lambda i: (i, 0)),
            pl.BlockSpec((None, 1, D), lambda i: (_mod_row(layer, cond_base + (i * tc) // rows_per_cond, 5), 0, 0)),
            pl.BlockSpec((1, D), lambda i: (0, 0)),
        ],
        out_specs=pl.BlockSpec((tc, D), lambda i: (i, 0)),
        out_shape=jax.ShapeDtypeStruct((m, D), F32),
        scratch_shapes=[pltpu.VMEM((2, tc, D), F32), pltpu.VMEM((2, tc, D), F32),
                        pltpu.SemaphoreType.DMA((2,)), pltpu.SemaphoreType.DMA((2,))],
        compiler_params=_params(1),
        name="moe_combine",
    )(d1_tiles, d2_tiles, d1_tiles, d2_tiles, y, h, w1b, w2b, mod3, fg)


def _moe_block(h, norm_g, mod3, layer, rows_per_cond, cond_base, wr_pad, bias_col, w_gate_up, w_down, final_g):
    t = h.shape[0]
    f, route, cnt = _route(h, norm_g, mod3, layer, rows_per_cond, cond_base, wr_pad, bias_col)

    e1 = route[0].astype(jnp.int32)
    e2 = route[1].astype(jnp.int32)
    counts = cnt[:, 0].astype(jnp.int32)
    tiles_per_e = (counts + TM_EXP - 1) // TM_EXP
    tile_end = jnp.cumsum(tiles_per_e)
    tile_start = tile_end - tiles_per_e
    row_start = tile_start * TM_EXP
    d1 = row_start[e1] + route[4].astype(jnp.int32)
    d2 = row_start[e2] + route[5].astype(jnp.int32)
    nt = 2 * t // TM_EXP + N_EXP
    tok = jnp.arange(t, dtype=jnp.int32)
    src = jnp.zeros((nt * TM_EXP,), jnp.int32).at[d1].set(tok).at[d2].set(tok)
    n_valid = tile_end[-1]
    tile_src = jnp.minimum(jnp.arange(nt, dtype=jnp.int32), n_valid - 1)
    tile_e = jnp.minimum(jnp.searchsorted(tile_end, tile_src, side="right"), N_EXP - 1).astype(jnp.int32)

    nf = D_FF // TF_EXP
    step_id = jnp.arange(nf * nt, dtype=jnp.int32)
    step_valid = step_id < nf * n_valid
    sid = jnp.minimum(step_id, nf * n_valid - 1)
    step_e = jnp.minimum(jnp.searchsorted(nf * tile_end, sid, side="right"), N_EXP - 1).astype(jnp.int32)
    local = sid - nf * tile_start[step_e]
    n_e = jnp.maximum(tiles_per_e[step_e], 1)
    step_f = local // n_e
    step_r = tile_start[step_e] + local % n_e
    pad = step_id - nf * n_valid
    steps = jnp.stack([step_e, step_f, step_r,
                       jnp.where(step_valid, step_r, n_valid + pad // nf),
                       jnp.where(step_valid, step_f, pad % nf),
                       step_valid.astype(jnp.int32)]).astype(jnp.int32)

    xs = _dispatch(f, src.reshape(nt, 1, TM_EXP), n_valid.reshape(1).astype(jnp.int32))
    act = _gate_up(xs, w_gate_up, layer, steps)
    y = _down(act, w_down, layer, tile_e, tile_src)

    ntc = t // TC_COMB
    w1b = jnp.broadcast_to(route[2][:, None], (t, LANE))
    w2b = jnp.broadcast_to(route[3][:, None], (t, LANE))
    return _combine(y, d1.reshape(ntc, 1, TC_COMB), d2.reshape(ntc, 1, TC_COMB), h, w1b, w2b,
                    mod3, layer, rows_per_cond, cond_base, final_g)


def _rope_tables(seq):
    rows = seq // GRID_W
    row = jnp.repeat(jnp.arange(rows), GRID_W).astype(F32)
    col = jnp.tile(jnp.arange(GRID_W), rows).astype(F32)
    n_freq = DH // 4
    inv_freq = ROPE_THETA ** (-jnp.arange(n_freq, dtype=F32) / n_freq)
    ang = jnp.concatenate([row[:, None] * inv_freq, col[:, None] * inv_freq], axis=-1)
    cos, sin = jnp.cos(ang), jnp.sin(ang)
    return jnp.concatenate([cos, cos], axis=-1), jnp.concatenate([-sin, sin], axis=-1)


def kernel(x, c, ctx, c_ctx, w_ada, b_ada, norm1_g, norm2_g, attn_w_qkv, attn_w_o, attn_sink, conv_w_in, conv_w,
           conv_w_out, w_router, router_bias, moe_w_gate_up, moe_w_down, final_g):
    batch, seq, _ = x.shape
    n_ctx = ctx.shape[1]
    depth = w_ada.shape[0]
    assert depth == 2 and x.shape[2] == D
    ctx_row = batch

    c8 = jnp.concatenate([c, c_ctx[None, :], jnp.zeros((SUBLANE - batch - 1, D), F32)], axis=0)
    mod3 = _ada(c8, w_ada, b_ada).reshape(depth * SUBLANE * N_MOD, 1, D)
    wr_pad = jnp.pad(w_router, ((0, 0), (0, LANE - N_EXP)))
    bias_col = router_bias.astype(F32).reshape(N_EXP, 1)
    rope = _rope_tables(seq)

    h_lat = x.reshape(batch * seq, D)
    h_ctx = ctx.reshape(batch * n_ctx, D)

    a_lat = _normmod(h_lat, norm1_g[0], mod3, 0, 0, seq, 0)
    a_ctx = _normmod(h_ctx, norm1_g[0], mod3, 0, 0, batch * n_ctx, ctx_row)
    qkv_lat = _qkv_proj(a_lat, attn_w_qkv[0], rope)
    qkv_ctx = _qkv_proj(a_ctx, attn_w_qkv[0])
    o_lat = _attention(qkv_lat, qkv_lat, qkv_ctx, attn_sink[0], batch, band=True)
    o_ctx = _attention(qkv_ctx, qkv_ctx, qkv_ctx, attn_sink[0], batch, band=False)
    h_lat = _proj_resid(o_lat, attn_w_o[0], h_lat, mod3, 0, 2, seq, 0)
    h_ctx = _proj_resid(o_ctx, attn_w_o[0], h_ctx, mod3, 0, 2, batch * n_ctx, ctx_row)
    moe_args = (wr_pad, bias_col, moe_w_gate_up, moe_w_down)
    h_lat = _moe_block(h_lat, norm2_g[0], mod3, 0, seq, 0, *moe_args, None)
    h_ctx = _moe_block(h_ctx, norm2_g[0], mod3, 0, batch * n_ctx, ctx_row, *moe_args, None)

    a_lat = _normmod(h_lat, norm1_g[1], mod3, 1, 0, seq, 0)
    b_gate, u = _conv_in(a_lat, conv_w_in[0])
    z = _conv_gate(b_gate, u, conv_w[0], seq)
    h_lat = _proj_resid(z, conv_w_out[0], h_lat, mod3, 1, 2, seq, 0)
    h_lat = _moe_block(h_lat, norm2_g[1], mod3, 1, seq, 0, *moe_args, final_g)
    return h_lat.reshape(batch, seq, D)
```

```python
import functools

import jax
import jax.numpy as jnp
from jax import lax
from jax.experimental import pallas as pl
from jax.experimental.pallas import tpu as pltpu

D = 4096
N_HEADS = 32
N_KV = 8
GQA = 4
DH = 128
QKV = (N_HEADS + 2 * N_KV) * DH
GRID_W = 64
WINDOW = 128
ROPE_THETA = 10000.0
N_EXP = 16
N_GRP = 4
EXP_PER_GRP = 4
D_FF = 1024
N_MOD = 6
EPS = 1e-6
NEG = -1e30

LANE = 128
SUBLANE = 8
VMEM_LIMIT = 56 * 1024 * 1024

TM_MM = 512
TN_MM = 1024
TN_CONV = 256
TM_ROW = 512
TM_CONV = 256
TM_EXP = 256
TF_EXP = 512
TC_COMB = 256

F32 = jnp.float32
BF16 = jnp.bfloat16


def _params(n_axes):
    return pltpu.CompilerParams(dimension_semantics=("arbitrary",) * n_axes,
                                vmem_limit_bytes=VMEM_LIMIT)


def _dot(a, b):
    return lax.dot_general(a, b, (((1,), (0,)), ((), ())), preferred_element_type=F32)


def _dot_nt(a, b):
    return lax.dot_general(a, b, (((1,), (1,)), ((), ())), preferred_element_type=F32)


def _ada_kernel(c_ref, w_ref, b_ref, o_ref):
    s = c_ref[...]
    s = (s * jax.nn.sigmoid(s)).astype(BF16)
    o_ref[...] = _dot(s, w_ref[...]) + b_ref[...]


def _ada(c8, w_ada, b_ada):
    depth = w_ada.shape[0]
    n = w_ada.shape[2]
    return pl.pallas_call(
        _ada_kernel,
        grid=(depth, n // TN_MM),
        in_specs=[
            pl.BlockSpec((SUBLANE, D), lambda l, j: (0, 0)),
            pl.BlockSpec((None, D, TN_MM), lambda l, j: (l, 0, j)),
            pl.BlockSpec((None, 1, TN_MM), lambda l, j: (l, 0, j)),
        ],
        out_specs=pl.BlockSpec((None, SUBLANE, TN_MM), lambda l, j: (l, 0, j)),
        out_shape=jax.ShapeDtypeStruct((depth, SUBLANE, n), F32),
        compiler_params=_params(2),
        name="ada_mod",
    )(c8, w_ada, b_ada.reshape(depth, 1, n))


def _mod_row(layer, cond_row, which):
    return (layer * SUBLANE + cond_row) * N_MOD + which


def _rms(x, g):
    return x * lax.rsqrt(jnp.mean(x * x, axis=-1, keepdims=True) + EPS) * g


def _normmod_kernel(h_ref, g_ref, sh_ref, sc_ref, o_ref):
    y = _rms(h_ref[...], g_ref[...])
    o_ref[...] = (y * (1.0 + sc_ref[...]) + sh_ref[...]).astype(o_ref.dtype)


def _normmod(h, g, mod3, layer, which_shift, rows_per_cond, cond_base):
    m = h.shape[0]
    tm = min(TM_ROW, rows_per_cond)

    def row(which):
        return lambda i: (_mod_row(layer, cond_base + (i * tm) // rows_per_cond, which), 0, 0)

    return pl.pallas_call(
        _normmod_kernel,
        grid=(m // tm,),
        in_specs=[
            pl.BlockSpec((tm, D), lambda i: (i, 0)),
            pl.BlockSpec((1, D), lambda i: (0, 0)),
            pl.BlockSpec((None, 1, D), row(which_shift)),
            pl.BlockSpec((None, 1, D), row(which_shift + 1)),
        ],
        out_specs=pl.BlockSpec((tm, D), lambda i: (i, 0)),
        out_shape=jax.ShapeDtypeStruct((m, D), BF16),
        compiler_params=_params(1),
        name="norm_modulate",
    )(h, g.reshape(1, D), mod3, mod3)


def _mm_plain_kernel(x_ref, w_ref, o_ref):
    o_ref[...] = _dot(x_ref[...], w_ref[...]).astype(o_ref.dtype)


def _mm_rope_kernel(x_ref, w_ref, cos_ref, sin_ref, o_ref, *, n_rope_tiles):
    acc = _dot(x_ref[...], w_ref[...])

    @pl.when(pl.program_id(0) < n_rope_tiles)
    def _():
        cosf = cos_ref[...]
        sinf = sin_ref[...]
        for hh in range(acc.shape[1] // DH):
            xh = acc[:, hh * DH:(hh + 1) * DH]
            o_ref[:, hh * DH:(hh + 1) * DH] = (
                xh * cosf + pltpu.roll(xh, DH // 2, axis=1) * sinf).astype(o_ref.dtype)

    @pl.when(pl.program_id(0) >= n_rope_tiles)
    def _():
        o_ref[...] = acc.astype(o_ref.dtype)


def _mm_resid_kernel(x_ref, w_ref, h_ref, gate_ref, o_ref):
    o_ref[...] = h_ref[...] + gate_ref[...] * _dot(x_ref[...], w_ref[...])


def _qkv_proj(a, w_qkv, rope=None):
    m = a.shape[0]
    grid = (QKV // TN_MM, m // TM_MM)
    x_spec = pl.BlockSpec((TM_MM, D), lambda j, i: (i, 0))
    w_spec = pl.BlockSpec((D, TN_MM), lambda j, i: (0, j))
    o_spec = pl.BlockSpec((TM_MM, TN_MM), lambda j, i: (i, j))
    out_shape = jax.ShapeDtypeStruct((m, QKV), BF16)
    if rope is None:
        return pl.pallas_call(_mm_plain_kernel, grid=grid, in_specs=[x_spec, w_spec], out_specs=o_spec,
                              out_shape=out_shape, compiler_params=_params(2), name="qkv_ctx")(a, w_qkv)
    cosf, sinf = rope
    seq_tiles = cosf.shape[0] // TM_MM
    t_spec = pl.BlockSpec((TM_MM, DH), lambda j, i: (i % seq_tiles, 0))
    n_rope_tiles = (N_HEADS + N_KV) * DH // TN_MM
    return pl.pallas_call(
        functools.partial(_mm_rope_kernel, n_rope_tiles=n_rope_tiles),
        grid=grid, in_specs=[x_spec, w_spec, t_spec, t_spec], out_specs=o_spec,
        out_shape=out_shape, compiler_params=_params(2), name="qkv_rope")(a, w_qkv, cosf, sinf)


def _proj_resid(x, w, h, mod3, layer, which_gate, rows_per_cond, cond_base):
    m = x.shape[0]
    tm = min(TM_MM, rows_per_cond)

    def gate_map(j, i):
        return (_mod_row(layer, cond_base + (i * tm) // rows_per_cond, which_gate), 0, j)

    return pl.pallas_call(
        _mm_resid_kernel,
        grid=(D // TN_MM, m // tm),
        in_specs=[
            pl.BlockSpec((tm, D), lambda j, i: (i, 0)),
            pl.BlockSpec((D, TN_MM), lambda j, i: (0, j)),
            pl.BlockSpec((tm, TN_MM), lambda j, i: (i, j)),
            pl.BlockSpec((None, 1, TN_MM), gate_map),
        ],
        out_specs=pl.BlockSpec((tm, TN_MM), lambda j, i: (i, j)),
        out_shape=jax.ShapeDtypeStruct((m, D), F32),
        compiler_params=_params(2),
        name="proj_residual",
    )(x, w, h, mod3)


def _attn_kernel(sink_ref, q_ref, k_ref, v_ref, kc_ref, vc_ref, o_ref, *, band, seq):
    kh = pl.program_id(1)
    scale = DH ** -0.5
    kc = kc_ref[...]
    vc = vc_ref[...]
    n_ctx = kc.shape[0]
    rows = GQA * WINDOW
    row = lax.broadcasted_iota(jnp.int32, (rows, 1), 0)
    sink_col = jnp.full((rows, 1), sink_ref[kh * GQA + GQA - 1], F32)
    for g in range(GQA - 2, -1, -1):
        sink_col = jnp.where(row < (g + 1) * WINDOW, sink_ref[kh * GQA + g], sink_col)
    n_band = 3 * WINDOW if band else 0

    def block(n, carry):
        q0 = pl.multiple_of(n * WINDOW, WINDOW)
        qb = q_ref[pl.ds(q0, WINDOW), :]
        q4 = jnp.concatenate([qb[:, g * DH:(g + 1) * DH] for g in range(GQA)], axis=0)
        if band:
            ws = pl.multiple_of(jnp.clip((n - 1) * WINDOW, 0, seq - n_band), WINDOW)
            keys = jnp.concatenate([k_ref[pl.ds(ws, n_band), :], kc], axis=0)
            vals = jnp.concatenate([v_ref[pl.ds(ws, n_band), :], vc], axis=0)
        else:
            keys, vals = kc, vc
        s = _dot_nt(q4, keys) * scale
        if band:
            q_pos = q0 + (row & (WINDOW - 1))
            col = lax.broadcasted_iota(jnp.int32, (1, n_band + n_ctx), 1)
            valid = (jnp.abs(q_pos - (ws + col)) <= WINDOW) | (col >= n_band)
            s = jnp.where(valid, s, NEG)
        m = jnp.maximum(jnp.max(s, axis=-1, keepdims=True), sink_col)
        p = jnp.exp(s - m)
        denom = jnp.sum(p, axis=-1, keepdims=True) + jnp.exp(sink_col - m)
        o = _dot(p.astype(BF16), vals) / denom
        o_ref[pl.ds(q0, WINDOW), :] = jnp.concatenate(
            [o[g * WINDOW:(g + 1) * WINDOW] for g in range(GQA)], axis=1).astype(o_ref.dtype)
        return carry

    lax.fori_loop(0, q_ref.shape[0] // WINDOW, block, 0)


def _attention(qkv_q, qkv_kv, qkv_ctx, sink, batch, band):
    seq_q = qkv_q.shape[0] // batch
    seq_k = qkv_kv.shape[0] // batch
    n_ctx = qkv_ctx.shape[0] // batch
    k_col = N_HEADS
    v_col = N_HEADS + N_KV
    return pl.pallas_call(
        functools.partial(_attn_kernel, band=band, seq=seq_k),
        grid_spec=pltpu.PrefetchScalarGridSpec(
            num_scalar_prefetch=1,
            grid=(batch, N_KV),
            in_specs=[
                pl.BlockSpec((seq_q, GQA * DH), lambda b, k, s: (b, k)),
                pl.BlockSpec((seq_k, DH), lambda b, k, s: (b, k_col + k)),
                pl.BlockSpec((seq_k, DH), lambda b, k, s: (b, v_col + k)),
                pl.BlockSpec((n_ctx, DH), lambda b, k, s: (b, k_col + k)),
                pl.BlockSpec((n_ctx, DH), lambda b, k, s: (b, v_col + k)),
            ],
            out_specs=pl.BlockSpec((seq_q, GQA * DH), lambda b, k, s: (b, k)),
        ),
        out_shape=jax.ShapeDtypeStruct((qkv_q.shape[0], D), BF16),
        compiler_params=_params(2),
        name="window_attn" if band else "ctx_attn",
    )(sink, qkv_q, qkv_kv, qkv_kv, qkv_ctx, qkv_ctx)


def _conv_in_kernel(x_ref, wb_ref, wc_ref, wx_ref, b_ref, u_ref):
    x = x_ref[...]
    b_ref[...] = _dot(x, wb_ref[...])
    u_ref[...] = _dot(x, wc_ref[...]) * _dot(x, wx_ref[...])


def _conv_in(a, w_in):
    m = a.shape[0]
    nt = D // TN_CONV
    o_spec = pl.BlockSpec((TM_MM, TN_CONV), lambda j, i: (i, j))
    return pl.pallas_call(
        _conv_in_kernel,
        grid=(nt, m // TM_MM),
        in_specs=[
            pl.BlockSpec((TM_MM, D), lambda j, i: (i, 0)),
            pl.BlockSpec((D, TN_CONV), lambda j, i: (0, j)),
            pl.BlockSpec((D, TN_CONV), lambda j, i: (0, nt + j)),
            pl.BlockSpec((D, TN_CONV), lambda j, i: (0, 2 * nt + j)),
        ],
        out_specs=[o_spec, o_spec],
        out_shape=[jax.ShapeDtypeStruct((m, D), F32)] * 2,
        compiler_params=_params(2),
        name="conv_in_proj",
    )(a, w_in, w_in, w_in)


def _conv_gate_kernel(b_ref, u_ref, up_ref, un_ref, w_ref, o_ref, *, seq):
    i = pl.program_id(0)
    tm = u_ref.shape[0]
    u = u_ref[...]
    t0 = i * tm
    prev_row = jnp.where(lax.rem(t0, seq) == 0, 0.0, up_ref[SUBLANE - 1:SUBLANE, :])
    next_row = jnp.where(lax.rem(t0 + tm, seq) == 0, 0.0, un_ref[0:1, :])
    row = lax.broadcasted_iota(jnp.int32, u.shape, 0)
    u_prev = jnp.where(row == 0, prev_row, pltpu.roll(u, 1, axis=0))
    u_next = jnp.where(row == tm - 1, next_row, pltpu.roll(u, tm - 1, axis=0))
    y = u_prev * w_ref[0:1, :] + u * w_ref[1:2, :] + u_next * w_ref[2:3, :]
    o_ref[...] = (b_ref[...] * y).astype(o_ref.dtype)


def _conv_gate(b, u, conv_w, seq):
    m = u.shape[0]
    tm = TM_CONV
    per = tm // SUBLANE
    last = m // SUBLANE - 1
    return pl.pallas_call(
        functools.partial(_conv_gate_kernel, seq=seq),
        grid=(m // tm,),
        in_specs=[
            pl.BlockSpec((tm, D), lambda i: (i, 0)),
            pl.BlockSpec((tm, D), lambda i: (i, 0)),
            pl.BlockSpec((SUBLANE, D), lambda i: (jnp.maximum(i * per - 1, 0), 0)),
            pl.BlockSpec((SUBLANE, D), lambda i: (jnp.minimum((i + 1) * per, last), 0)),
            pl.BlockSpec((3, D), lambda i: (0, 0)),
        ],
        out_specs=pl.BlockSpec((tm, D), lambda i: (i, 0)),
        out_shape=jax.ShapeDtypeStruct((m, D), BF16),
        compiler_params=_params(1),
        name="conv_gate",
    )(b, u, u, u, conv_w)


def _route_kernel(h_ref, g_ref, sh_ref, sc_ref, wr_ref, bias_ref, f_ref, route_ref, cnt_ref, run_ref):
    i = pl.program_id(0)
    tm = h_ref.shape[0]

    @pl.when(i == 0)
    def _():
        run_ref[...] = jnp.zeros_like(run_ref)

    f = _rms(h_ref[...], g_ref[...]) * (1.0 + sc_ref[...]) + sh_ref[...]
    f_ref[...] = f
    logits = jnp.dot(f, wr_ref[...], preferred_element_type=F32, precision=lax.Precision.HIGHEST)
    lt = logits.T[:N_EXP]
    score = jax.nn.sigmoid(lt)
    sel = score + bias_ref[...]
    sel_rows = [sel[e:e + 1] for e in range(N_EXP)]
    score_rows = [score[e:e + 1] for e in range(N_EXP)]

    grp_scores = []
    for gi in range(N_GRP):
        a, b, c, d = sel_rows[gi * EXP_PER_GRP:(gi + 1) * EXP_PER_GRP]
        hi1, lo1 = jnp.maximum(a, b), jnp.minimum(a, b)
        hi2, lo2 = jnp.maximum(c, d), jnp.minimum(c, d)
        grp_scores.append(jnp.maximum(hi1, hi2) + jnp.maximum(jnp.minimum(hi1, hi2), jnp.maximum(lo1, lo2)))
    best = grp_scores[0]
    g_idx = jnp.zeros_like(best, dtype=jnp.int32)
    for gi in range(1, N_GRP):
        better = grp_scores[gi] > best
        g_idx = jnp.where(better, gi, g_idx)
        best = jnp.where(better, grp_scores[gi], best)

    masked = [jnp.where(g_idx == e // EXP_PER_GRP, sel_rows[e], NEG) for e in range(N_EXP)]
    v1 = masked[0]
    e1 = jnp.zeros_like(g_idx)
    for e in range(1, N_EXP):
        better = masked[e] > v1
        e1 = jnp.where(better, e, e1)
        v1 = jnp.where(better, masked[e], v1)
    v2 = jnp.full_like(v1, -jnp.inf)
    e2 = jnp.zeros_like(g_idx)
    for e in range(N_EXP):
        better = (masked[e] > v2) & (e1 != e)
        e2 = jnp.where(better, e, e2)
        v2 = jnp.where(better, masked[e], v2)
    s1 = jnp.zeros_like(v1)
    s2 = jnp.zeros_like(v1)
    for e in range(N_EXP):
        s1 = jnp.where(e1 == e, score_rows[e], s1)
        s2 = jnp.where(e2 == e, score_rows[e], s2)
    tot = s1 + s2
    w1 = s1 / tot
    w2 = s2 / tot

    e_iota = lax.broadcasted_iota(jnp.int32, (N_EXP, tm), 0)
    hit = (e_iota == e1) | (e_iota == e2)
    onehot = jnp.where(hit, 1.0, 0.0).astype(BF16)
    r_i = lax.broadcasted_iota(jnp.int32, (tm, tm), 0)
    c_i = lax.broadcasted_iota(jnp.int32, (tm, tm), 1)
    upper = jnp.where(r_i <= c_i, 1.0, 0.0).astype(BF16)
    prefix = _dot(onehot, upper)
    rank = prefix - 1.0 + run_ref[:, 0:1]
    rank1 = jnp.sum(jnp.where(e_iota == e1, rank, 0.0), axis=0, keepdims=True)
    rank2 = jnp.sum(jnp.where(e_iota == e2, rank, 0.0), axis=0, keepdims=True)
    run_ref[...] = run_ref[...] + jnp.sum(jnp.where(hit, 1.0, 0.0), axis=1, keepdims=True)
    cnt_ref[...] = run_ref[...]

    route_ref[0:1, :] = e1.astype(F32)
    route_ref[1:2, :] = e2.astype(F32)
    route_ref[2:3, :] = w1
    route_ref[3:4, :] = w2
    route_ref[4:5, :] = rank1
    route_ref[5:6, :] = rank2
    route_ref[6:8, :] = jnp.zeros((2, tm), F32)


def _route(h, g, mod3, layer, rows_per_cond, cond_base, wr_pad, bias_col):
    m = h.shape[0]
    tm = min(TM_ROW, rows_per_cond)

    def row(which):
        return lambda i: (_mod_row(layer, cond_base + (i * tm) // rows_per_cond, which), 0, 0)

    return pl.pallas_call(
        _route_kernel,
        grid=(m // tm,),
        in_specs=[
            pl.BlockSpec((tm, D), lambda i: (i, 0)),
            pl.BlockSpec((1, D), lambda i: (0, 0)),
            pl.BlockSpec((None, 1, D), row(3)),
            pl.BlockSpec((None, 1, D), row(4)),
            pl.BlockSpec((D, LANE), lambda i: (0, 0)),
            pl.BlockSpec((N_EXP, 1), lambda i: (0, 0)),
        ],
        out_specs=[
            pl.BlockSpec((tm, D), lambda i: (i, 0)),
            pl.BlockSpec((SUBLANE, tm), lambda i: (0, i)),
            pl.BlockSpec((N_EXP, LANE), lambda i: (0, 0)),
        ],
        out_shape=[
            jax.ShapeDtypeStruct((m, D), F32),
            jax.ShapeDtypeStruct((SUBLANE, m), F32),
            jax.ShapeDtypeStruct((N_EXP, LANE), F32),
        ],
        scratch_shapes=[pltpu.VMEM((N_EXP, LANE), F32)],
        compiler_params=_params(1),
        name="norm_route",
    )(h, g.reshape(1, D), mod3, mod3, wr_pad, bias_col)


def _row_gather(idx_ref, src_hbm, buf, sem, slot, n_rows):
    def issue(k, carry):
        pltpu.make_async_copy(src_hbm.at[pl.ds(idx_ref[0, k], 1)], buf.at[slot, pl.ds(k, 1)], sem.at[slot]).start()
        return carry
    lax.fori_loop(0, n_rows, issue, 0)


def _row_gather_wait(src_hbm, buf, sem, slot, n_rows):
    pltpu.make_async_copy(src_hbm.at[pl.ds(0, n_rows)], buf.at[slot], sem.at[slot]).wait()


def _dispatch_kernel(nvalid_ref, idx_ref, idx_next_ref, f_hbm, o_ref, buf, sem):
    i = pl.program_id(0)
    tm = o_ref.shape[0]
    n_valid = nvalid_ref[0]
    slot = lax.rem(i, 2)

    @pl.when((i == 0) & (n_valid > 0))
    def _():
        _row_gather(idx_ref, f_hbm, buf, sem, 0, tm)

    @pl.when(i + 1 < n_valid)
    def _():
        _row_gather(idx_next_ref, f_hbm, buf, sem, 1 - slot, tm)

    @pl.when(i < n_valid)
    def _():
        _row_gather_wait(f_hbm, buf, sem, slot, tm)
        o_ref[...] = buf[slot].astype(o_ref.dtype)

    @pl.when(i >= n_valid)
    def _():
        o_ref[...] = jnp.zeros_like(o_ref)


def _dispatch(f, src_tiles, n_valid_tiles):
    nt = src_tiles.shape[0]
    idx_spec = pl.BlockSpec((None, 1, TM_EXP), lambda i, nv: (i, 0, 0), memory_space=pltpu.SMEM)
    idx_next_spec = pl.BlockSpec((None, 1, TM_EXP), lambda i, nv: (jnp.minimum(i + 1, nt - 1), 0, 0),
                                 memory_space=pltpu.SMEM)
    return pl.pallas_call(
        _dispatch_kernel,
        grid_spec=pltpu.PrefetchScalarGridSpec(
            num_scalar_prefetch=1,
            grid=(nt,),
            in_specs=[idx_spec, idx_next_spec, pl.BlockSpec(memory_space=pl.ANY)],
            out_specs=pl.BlockSpec((TM_EXP, D), lambda i, nv: (i, 0)),
            scratch_shapes=[pltpu.VMEM((2, TM_EXP, D), F32), pltpu.SemaphoreType.DMA((2,))],
        ),
        out_shape=jax.ShapeDtypeStruct((nt * TM_EXP, D), BF16),
        compiler_params=_params(1),
        name="moe_dispatch",
    )(n_valid_tiles, src_tiles, src_tiles, f)


(ST_EXPERT, ST_WCOL, ST_XROW, ST_OROW, ST_OCOL, ST_VALID, ST_FIRST, ST_SLOT, ST_NEXT_E, ST_NEXT_WCOL,
 ST_HAS_NEXT) = range(11)


def _gate_up_weights(w_hbm, wbuf, sem, layer, e, wcol, slot):
    nf = D_FF // TF_EXP
    col_g = pl.multiple_of(wcol * TF_EXP, TF_EXP)
    col_u = pl.multiple_of((nf + wcol) * TF_EXP, TF_EXP)
    return (pltpu.make_async_copy(w_hbm.at[layer, e, :, pl.ds(col_g, TF_EXP)], wbuf.at[slot, 0], sem.at[slot, 0]),
            pltpu.make_async_copy(w_hbm.at[layer, e, :, pl.ds(col_u, TF_EXP)], wbuf.at[slot, 1], sem.at[slot, 1]))


def _gate_up_kernel(st_ref, x_ref, w_hbm, o_ref, wbuf, sem, *, layer):
    s = pl.program_id(0)
    valid = st_ref[ST_VALID, s] > 0
    slot = st_ref[ST_SLOT, s]

    @pl.when(valid & (st_ref[ST_FIRST, s] > 0))
    def _():
        cur = _gate_up_weights(w_hbm, wbuf, sem, layer, st_ref[ST_EXPERT, s], st_ref[ST_WCOL, s], slot)

        @pl.when(s == 0)
        def _():
            for c in cur:
                c.start()

        for c in cur:
            c.wait()

        @pl.when(st_ref[ST_HAS_NEXT, s] > 0)
        def _():
            for c in _gate_up_weights(w_hbm, wbuf, sem, layer, st_ref[ST_NEXT_E, s], st_ref[ST_NEXT_WCOL, s],
                                      1 - slot):
                c.start()

    @pl.when(valid)
    def _():
        x = x_ref[...]
        gate = _dot(x, wbuf[slot, 0])
        up = _dot(x, wbuf[slot, 1])
        o_ref[...] = (gate * jax.nn.sigmoid(gate) * up).astype(o_ref.dtype)

    @pl.when(jnp.logical_not(valid))
    def _():
        o_ref[...] = jnp.zeros_like(o_ref)


def _gate_up(xs, w_gate_up, layer, steps):
    n_steps = steps.shape[1]
    return pl.pallas_call(
        functools.partial(_gate_up_kernel, layer=layer),
        grid_spec=pltpu.PrefetchScalarGridSpec(
            num_scalar_prefetch=1,
            grid=(n_steps,),
            in_specs=[
                pl.BlockSpec((TM_EXP, D), lambda s, st: (st[ST_XROW, s], 0)),
                pl.BlockSpec(memory_space=pl.ANY),
            ],
            out_specs=pl.BlockSpec((TM_EXP, TF_EXP), lambda s, st: (st[ST_OROW, s], st[ST_OCOL, s])),
            scratch_shapes=[pltpu.VMEM((2, 2, D, TF_EXP), F32), pltpu.SemaphoreType.DMA((2, 2))],
        ),
        out_shape=jax.ShapeDtypeStruct((xs.shape[0], D_FF), BF16),
        compiler_params=_params(1),
        name="moe_gate_up",
    )(steps, xs, w_gate_up)


TL_EXPERT, TL_SRC, TL_FIRST, TL_SLOT, TL_NEXT_E, TL_HAS_NEXT = range(6)


def _down_weights(w_hbm, wbuf, sem, layer, e, slot):
    return pltpu.make_async_copy(w_hbm.at[layer, e], wbuf.at[slot], sem.at[slot])


def _down_kernel(tl_ref, a_ref, w_hbm, o_ref, wbuf, sem, *, layer):
    i = pl.program_id(0)
    valid = tl_ref[TL_SRC, i] == i
    slot = tl_ref[TL_SLOT, i]

    @pl.when(valid & (tl_ref[TL_FIRST, i] > 0))
    def _():
        cur = _down_weights(w_hbm, wbuf, sem, layer, tl_ref[TL_EXPERT, i], slot)

        @pl.when(i == 0)
        def _():
            cur.start()

        cur.wait()

        @pl.when(tl_ref[TL_HAS_NEXT, i] > 0)
        def _():
            _down_weights(w_hbm, wbuf, sem, layer, tl_ref[TL_NEXT_E, i], 1 - slot).start()

    @pl.when(valid)
    def _():
        o_ref[...] = _dot(a_ref[...], wbuf[slot])

    @pl.when(jnp.logical_not(valid))
    def _():
        o_ref[...] = jnp.zeros_like(o_ref)


def _down(act, w_down, layer, tiles):
    nt = tiles.shape[1]
    return pl.pallas_call(
        functools.partial(_down_kernel, layer=layer),
        grid_spec=pltpu.PrefetchScalarGridSpec(
            num_scalar_prefetch=1,
            grid=(nt,),
            in_specs=[
                pl.BlockSpec((TM_EXP, D_FF), lambda i, tl: (tl[TL_SRC, i], 0)),
                pl.BlockSpec(memory_space=pl.ANY),
            ],
            out_specs=pl.BlockSpec((TM_EXP, D), lambda i, tl: (i, 0)),
            scratch_shapes=[pltpu.VMEM((2, D_FF, D), F32), pltpu.SemaphoreType.DMA((2,))],
        ),
        out_shape=jax.ShapeDtypeStruct((act.shape[0], D), F32),
        compiler_params=_params(1),
        name="moe_down",
    )(tiles, act, w_down)


def _combine_kernel(d1_ref, d2_ref, d1n_ref, d2n_ref, y_hbm, h_ref, w1_ref, w2_ref, gate_ref, fg_ref, o_ref,
                    buf1, buf2, sem1, sem2, *, final_norm):
    i = pl.program_id(0)
    n = pl.num_programs(0)
    tc = o_ref.shape[0]
    slot = lax.rem(i, 2)

    @pl.when(i == 0)
    def _():
        _row_gather(d1_ref, y_hbm, buf1, sem1, 0, tc)
        _row_gather(d2_ref, y_hbm, buf2, sem2, 0, tc)

    @pl.when(i + 1 < n)
    def _():
        _row_gather(d1n_ref, y_hbm, buf1, sem1, 1 - slot, tc)
        _row_gather(d2n_ref, y_hbm, buf2, sem2, 1 - slot, tc)

    _row_gather_wait(y_hbm, buf1, sem1, slot, tc)
    _row_gather_wait(y_hbm, buf2, sem2, slot, tc)
    moe = w1_ref[:, 0:1] * buf1[slot] + w2_ref[:, 0:1] * buf2[slot]
    out = h_ref[...] + gate_ref[...] * moe
    if final_norm:
        out = _rms(out, fg_ref[...])
    o_ref[...] = out


def _combine(y, d1_tiles, d2_tiles, h, w1b, w2b, mod3, layer, rows_per_cond, cond_base, final_g):
    m = h.shape[0]
    tc = TC_COMB
    nt = m // tc
    cur = pl.BlockSpec((None, 1, tc), lambda i: (i, 0, 0), memory_space=pltpu.SMEM)
    nxt = pl.BlockSpec((None, 1, tc), lambda i: (jnp.minimum(i + 1, nt - 1), 0, 0), memory_space=pltpu.SMEM)
    fg = jnp.ones((1, D), F32) if final_g is None else final_g.reshape(1, D)
    return pl.pallas_call(
        functools.partial(_combine_kernel, final_norm=final_g is not None),
        grid=(nt,),
        in_specs=[
            cur, cur, nxt, nxt,
            pl.BlockSpec(memory_space=pl.ANY),
            pl.BlockSpec((tc, D), lambda i: (i, 0)),
            pl.BlockSpec((tc, LANE), lambda i: (i, 0)),
            pl.BlockSpec((tc, LANE), lambda i: (i, 0)),
            pl.BlockSpec((None, 1, D), lambda i: (_mod_row(layer, cond_base + (i * tc) // rows_per_cond, 5), 0, 0)),
            pl.BlockSpec((1, D), lambda i: (0, 0)),
        ],
        out_specs=pl.BlockSpec((tc, D), lambda i: (i, 0)),
        out_shape=jax.ShapeDtypeStruct((m, D), F32),
        scratch_shapes=[pltpu.VMEM((2, tc, D), F32), pltpu.VMEM((2, tc, D), F32),
                        pltpu.SemaphoreType.DMA((2,)), pltpu.SemaphoreType.DMA((2,))],
        compiler_params=_params(1),
        name="moe_combine",
    )(d1_tiles, d2_tiles, d1_tiles, d2_tiles, y, h, w1b, w2b, mod3, fg)


def _moe_block(h, norm_g, mod3, layer, rows_per_cond, cond_base, wr_pad, bias_col, w_gate_up, w_down, final_g):
    t = h.shape[0]
    f, route, cnt = _route(h, norm_g, mod3, layer, rows_per_cond, cond_base, wr_pad, bias_col)

    i32 = jnp.int32
    e_ids = jnp.arange(N_EXP, dtype=i32)

    def take(table, idx):
        return jnp.sum(jnp.where(idx[:, None] == e_ids[None, :], table[None, :], 0), axis=1).astype(i32)

    def bucket(ends, pos):
        return jnp.sum((ends[None, :] <= pos[:, None]).astype(i32), axis=1)

    e1 = route[0].astype(i32)
    e2 = route[1].astype(i32)
    counts = cnt[:, 0].astype(i32)
    tiles_per_e = (counts + TM_EXP - 1) // TM_EXP
    tile_end = jnp.cumsum(tiles_per_e).astype(i32)
    tile_start = tile_end - tiles_per_e
    row_start = tile_start * TM_EXP
    d1 = take(row_start, e1) + route[4].astype(i32)
    d2 = take(row_start, e2) + route[5].astype(i32)
    nt = 2 * t // TM_EXP + N_EXP
    tok = jnp.arange(t, dtype=i32)
    src = jnp.zeros((nt * TM_EXP,), i32).at[jnp.concatenate([d1, d2])].set(jnp.concatenate([tok, tok]))
    n_valid = tile_end[-1]

    nonempty = tiles_per_e > 0
    order = (jnp.cumsum(nonempty.astype(i32)) - nonempty.astype(i32)).astype(i32)
    later = nonempty[None, :] & (e_ids[None, :] > e_ids[:, None])
    next_e = jnp.min(jnp.where(later, e_ids[None, :], N_EXP), axis=1).astype(i32)

    tile_id = jnp.arange(nt, dtype=i32)
    tile_src = jnp.minimum(tile_id, n_valid - 1)
    tile_e = jnp.minimum(bucket(tile_end, tile_src), N_EXP - 1)
    tile_next = take(next_e, tile_e)
    tiles = jnp.stack([
        tile_e, tile_src,
        (tile_src == take(tile_start, tile_e)).astype(i32),
        take(order, tile_e) % 2,
        jnp.minimum(tile_next, N_EXP - 1),
        (tile_next < N_EXP).astype(i32),
    ]).astype(i32)

    nf = D_FF // TF_EXP
    step_id = jnp.arange(nf * nt, dtype=i32)
    step_valid = step_id < nf * n_valid
    sid = jnp.minimum(step_id, nf * n_valid - 1)
    step_e = jnp.minimum(bucket(nf * tile_end, sid), N_EXP - 1)
    local = sid - nf * take(tile_start, step_e)
    n_e = jnp.maximum(take(tiles_per_e, step_e), 1)
    step_f = local // n_e
    step_r = take(tile_start, step_e) + local % n_e
    pad = step_id - nf * n_valid
    last_col = step_f == nf - 1
    step_next_e = jnp.where(last_col, take(next_e, step_e), step_e)
    steps = jnp.stack([
        step_e, step_f, step_r,
        jnp.where(step_valid, step_r, n_valid + pad // nf),
        jnp.where(step_valid, step_f, pad % nf),
        step_valid.astype(i32),
        (local % n_e == 0).astype(i32),
        (nf * take(order, step_e) + step_f) % 2,
        jnp.minimum(step_next_e, N_EXP - 1),
        jnp.where(last_col, 0, step_f + 1),
        (step_next_e < N_EXP).astype(i32),
    ]).astype(i32)

    xs = _dispatch(f, src.reshape(nt, 1, TM_EXP), n_valid.reshape(1).astype(i32))
    act = _gate_up(xs, w_gate_up, layer, steps)
    y = _down(act, w_down, layer, tiles)

    ntc = t // TC_COMB
    w1b = jnp.broadcast_to(route[2][:, None], (t, LANE))
    w2b = jnp.broadcast_to(route[3][:, None], (t, LANE))
    return _combine(y, d1.reshape(ntc, 1, TC_COMB), d2.reshape(ntc, 1, TC_COMB), h, w1b, w2b,
                    mod3, layer, rows_per_cond, cond_base, final_g)


def _rope_tables(seq):
    rows = seq // GRID_W
    row = jnp.repeat(jnp.arange(rows), GRID_W).astype(F32)
    col = jnp.tile(jnp.arange(GRID_W), rows).astype(F32)
    n_freq = DH // 4
    inv_freq = ROPE_THETA ** (-jnp.arange(n_freq, dtype=F32) / n_freq)
    ang = jnp.concatenate([row[:, None] * inv_freq, col[:, None] * inv_freq], axis=-1)
    cos, sin = jnp.cos(ang), jnp.sin(ang)
    return jnp.concatenate([cos, cos], axis=-1), jnp.concatenate([-sin, sin], axis=-1)


def kernel(x, c, ctx, c_ctx, w_ada, b_ada, norm1_g, norm2_g, attn_w_qkv, attn_w_o, attn_sink, conv_w_in, conv_w,
           conv_w_out, w_router, router_bias, moe_w_gate_up, moe_w_down, final_g):
    batch, seq, _ = x.shape
    n_ctx = ctx.shape[1]
    depth = w_ada.shape[0]
    assert depth == 2 and x.shape[2] == D
    ctx_row = batch

    c8 = jnp.concatenate([c, c_ctx[None, :], jnp.zeros((SUBLANE - batch - 1, D), F32)], axis=0)
    mod3 = _ada(c8, w_ada, b_ada).reshape(depth * SUBLANE * N_MOD, 1, D)
    wr_pad = jnp.pad(w_router, ((0, 0), (0, LANE - N_EXP)))
    bias_col = router_bias.astype(F32).reshape(N_EXP, 1)
    rope = _rope_tables(seq)

    h_lat = x.reshape(batch * seq, D)
    h_ctx = ctx.reshape(batch * n_ctx, D)

    a_lat = _normmod(h_lat, norm1_g[0], mod3, 0, 0, seq, 0)
    a_ctx = _normmod(h_ctx, norm1_g[0], mod3, 0, 0, batch * n_ctx, ctx_row)
    qkv_lat = _qkv_proj(a_lat, attn_w_qkv[0], rope)
    qkv_ctx = _qkv_proj(a_ctx, attn_w_qkv[0])
    o_lat = _attention(qkv_lat, qkv_lat, qkv_ctx, attn_sink[0], batch, band=True)
    o_ctx = _attention(qkv_ctx, qkv_ctx, qkv_ctx, attn_sink[0], batch, band=False)
    h_lat = _proj_resid(o_lat, attn_w_o[0], h_lat, mod3, 0, 2, seq, 0)
    h_ctx = _proj_resid(o_ctx, attn_w_o[0], h_ctx, mod3, 0, 2, batch * n_ctx, ctx_row)
    moe_args = (wr_pad, bias_col, moe_w_gate_up, moe_w_down)
    h_lat = _moe_block(h_lat, norm2_g[0], mod3, 0, seq, 0, *moe_args, None)
    h_ctx = _moe_block(h_ctx, norm2_g[0], mod3, 0, batch * n_ctx, ctx_row, *moe_args, None)

    a_lat = _normmod(h_lat, norm1_g[1], mod3, 1, 0, seq, 0)
    b_gate, u = _conv_in(a_lat, conv_w_in[0])
    z = _conv_gate(b_gate, u, conv_w[0], seq)
    h_lat = _proj_resid(z, conv_w_out[0], h_lat, mod3, 1, 2, seq, 0)
    h_lat = _moe_block(h_lat, norm2_g[1], mod3, 1, seq, 0, *moe_args, final_g)
    return h_lat.reshape(batch, seq, D)
```

```python
import functools

import jax
import jax.numpy as jnp
from jax import lax
from jax.experimental import pallas as pl
from jax.experimental.pallas import tpu as pltpu

D = 4096
N_HEADS = 32
N_KV = 8
GQA = 4
DH = 128
QKV = (N_HEADS + 2 * N_KV) * DH
GRID_W = 64
WINDOW = 128
ROPE_THETA = 10000.0
N_EXP = 16
N_GRP = 4
EXP_PER_GRP = 4
D_FF = 1024
N_MOD = 6
EPS = 1e-6
NEG = -1e30

LANE = 128
SUBLANE = 8
VMEM_LIMIT = 56 * 1024 * 1024
N_DMA_QUEUES = 2
WEIGHT_QUEUE = 1

TM_MM = 512
TN_MM = 1024
TN_CONV = 256
TM_ROW = 512
TM_CONV = 256
TM_EXP = 256
TF_EXP = 512
TC_COMB = 256

F32 = jnp.float32
BF16 = jnp.bfloat16


def _params(n_axes):
    return pltpu.CompilerParams(dimension_semantics=("arbitrary",) * n_axes,
                                vmem_limit_bytes=VMEM_LIMIT)


def _dot(a, b):
    return lax.dot_general(a, b, (((1,), (0,)), ((), ())), preferred_element_type=F32)


def _dot_nt(a, b):
    return lax.dot_general(a, b, (((1,), (1,)), ((), ())), preferred_element_type=F32)


def _ada_kernel(c_ref, w_ref, b_ref, o_ref):
    s = c_ref[...]
    s = (s * jax.nn.sigmoid(s)).astype(BF16)
    o_ref[...] = _dot(s, w_ref[...]) + b_ref[...]


def _ada(c8, w_ada, b_ada):
    depth = w_ada.shape[0]
    n = w_ada.shape[2]
    return pl.pallas_call(
        _ada_kernel,
        grid=(depth, n // TN_MM),
        in_specs=[
            pl.BlockSpec((SUBLANE, D), lambda l, j: (0, 0)),
            pl.BlockSpec((None, D, TN_MM), lambda l, j: (l, 0, j)),
            pl.BlockSpec((None, 1, TN_MM), lambda l, j: (l, 0, j)),
        ],
        out_specs=pl.BlockSpec((None, SUBLANE, TN_MM), lambda l, j: (l, 0, j)),
        out_shape=jax.ShapeDtypeStruct((depth, SUBLANE, n), F32),
        compiler_params=_params(2),
        name="ada_mod",
    )(c8, w_ada, b_ada.reshape(depth, 1, n))


def _mod_row(layer, cond_row, which):
    return (layer * SUBLANE + cond_row) * N_MOD + which


def _rms(x, g):
    return x * lax.rsqrt(jnp.mean(x * x, axis=-1, keepdims=True) + EPS) * g


def _normmod_kernel(h_ref, g_ref, sh_ref, sc_ref, o_ref):
    y = _rms(h_ref[...], g_ref[...])
    o_ref[...] = (y * (1.0 + sc_ref[...]) + sh_ref[...]).astype(o_ref.dtype)


def _normmod(h, g, mod3, layer, which_shift, rows_per_cond, cond_base):
    m = h.shape[0]
    tm = min(TM_ROW, rows_per_cond)

    def row(which):
        return lambda i: (_mod_row(layer, cond_base + (i * tm) // rows_per_cond, which), 0, 0)

    return pl.pallas_call(
        _normmod_kernel,
        grid=(m // tm,),
        in_specs=[
            pl.BlockSpec((tm, D), lambda i: (i, 0)),
            pl.BlockSpec((1, D), lambda i: (0, 0)),
            pl.BlockSpec((None, 1, D), row(which_shift)),
            pl.BlockSpec((None, 1, D), row(which_shift + 1)),
        ],
        out_specs=pl.BlockSpec((tm, D), lambda i: (i, 0)),
        out_shape=jax.ShapeDtypeStruct((m, D), BF16),
        compiler_params=_params(1),
        name="norm_modulate",
    )(h, g.reshape(1, D), mod3, mod3)


def _mm_plain_kernel(x_ref, w_ref, o_ref):
    o_ref[...] = _dot(x_ref[...], w_ref[...]).astype(o_ref.dtype)


def _mm_rope_kernel(x_ref, w_ref, cos_ref, sin_ref, o_ref, *, n_rope_tiles):
    acc = _dot(x_ref[...], w_ref[...])

    @pl.when(pl.program_id(0) < n_rope_tiles)
    def _():
        cosf = cos_ref[...]
        sinf = sin_ref[...]
        for hh in range(acc.shape[1] // DH):
            xh = acc[:, hh * DH:(hh + 1) * DH]
            o_ref[:, hh * DH:(hh + 1) * DH] = (
                xh * cosf + pltpu.roll(xh, DH // 2, axis=1) * sinf).astype(o_ref.dtype)

    @pl.when(pl.program_id(0) >= n_rope_tiles)
    def _():
        o_ref[...] = acc.astype(o_ref.dtype)


def _mm_resid_kernel(x_ref, w_ref, h_ref, gate_ref, o_ref):
    o_ref[...] = h_ref[...] + gate_ref[...] * _dot(x_ref[...], w_ref[...])


def _qkv_proj(a, w_qkv, rope=None):
    m = a.shape[0]
    grid = (QKV // TN_MM, m // TM_MM)
    x_spec = pl.BlockSpec((TM_MM, D), lambda j, i: (i, 0))
    w_spec = pl.BlockSpec((D, TN_MM), lambda j, i: (0, j))
    o_spec = pl.BlockSpec((TM_MM, TN_MM), lambda j, i: (i, j))
    out_shape = jax.ShapeDtypeStruct((m, QKV), BF16)
    if rope is None:
        return pl.pallas_call(_mm_plain_kernel, grid=grid, in_specs=[x_spec, w_spec], out_specs=o_spec,
                              out_shape=out_shape, compiler_params=_params(2), name="qkv_ctx")(a, w_qkv)
    cosf, sinf = rope
    seq_tiles = cosf.shape[0] // TM_MM
    t_spec = pl.BlockSpec((TM_MM, DH), lambda j, i: (i % seq_tiles, 0))
    n_rope_tiles = (N_HEADS + N_KV) * DH // TN_MM
    return pl.pallas_call(
        functools.partial(_mm_rope_kernel, n_rope_tiles=n_rope_tiles),
        grid=grid, in_specs=[x_spec, w_spec, t_spec, t_spec], out_specs=o_spec,
        out_shape=out_shape, compiler_params=_params(2), name="qkv_rope")(a, w_qkv, cosf, sinf)


def _proj_resid(x, w, h, mod3, layer, which_gate, rows_per_cond, cond_base):
    m = x.shape[0]
    tm = min(TM_MM, rows_per_cond)

    def gate_map(j, i):
        return (_mod_row(layer, cond_base + (i * tm) // rows_per_cond, which_gate), 0, j)

    return pl.pallas_call(
        _mm_resid_kernel,
        grid=(D // TN_MM, m // tm),
        in_specs=[
            pl.BlockSpec((tm, D), lambda j, i: (i, 0)),
            pl.BlockSpec((D, TN_MM), lambda j, i: (0, j)),
            pl.BlockSpec((tm, TN_MM), lambda j, i: (i, j)),
            pl.BlockSpec((None, 1, TN_MM), gate_map),
        ],
        out_specs=pl.BlockSpec((tm, TN_MM), lambda j, i: (i, j)),
        out_shape=jax.ShapeDtypeStruct((m, D), F32),
        compiler_params=_params(2),
        name="proj_residual",
    )(x, w, h, mod3)


def _attn_kernel(sink_ref, q_ref, k_ref, v_ref, kc_ref, vc_ref, o_ref, *scratch, band, seq):
    kh = pl.program_id(1)
    log2e = 1.4426950408889634
    c = DH ** -0.5 * log2e
    kc = kc_ref[...]
    vc = vc_ref[...]
    cols = GQA * WINDOW
    lane = lax.broadcasted_iota(jnp.int32, (1, cols), 1)
    sink_row = jnp.full((1, cols), sink_ref[kh * GQA + GQA - 1], F32)
    for g in range(GQA - 2, -1, -1):
        sink_row = jnp.where(lane < (g + 1) * WINDOW, sink_ref[kh * GQA + g], sink_row)
    sink_row = sink_row * log2e
    n_band = 3 * WINDOW if band else 0
    if band:
        (mask_ref,) = scratch
        key_minus_query = (lax.broadcasted_iota(jnp.int32, (n_band, cols), 0)
                           - (lax.broadcasted_iota(jnp.int32, (n_band, cols), 1) & (WINDOW - 1)))
        for t in range(3):
            valid = (key_minus_query >= (t - 1) * WINDOW) & (key_minus_query <= (t + 1) * WINDOW)
            mask_ref[t] = jnp.where(valid, 0.0, NEG)

    def block(n, carry):
        q0 = pl.multiple_of(n * WINDOW, WINDOW)
        qb = q_ref[pl.ds(q0, WINDOW), :]
        q4 = jnp.concatenate([qb[:, g * DH:(g + 1) * DH] for g in range(GQA)], axis=0)
        s_ctx = _dot_nt(kc, q4)
        m_raw = jnp.max(s_ctx, axis=0, keepdims=True)
        if band:
            ws = pl.multiple_of(jnp.clip((n - 1) * WINDOW, 0, seq - n_band), WINDOW)
            s_band = _dot_nt(k_ref[pl.ds(ws, n_band), :], q4) + mask_ref[(q0 - ws) // WINDOW]
            m_raw = jnp.maximum(m_raw, jnp.max(s_band, axis=0, keepdims=True))
        m = jnp.maximum(m_raw * c, sink_row)
        p_ctx = jnp.exp2(s_ctx * c - m)
        denom = jnp.sum(p_ctx, axis=0, keepdims=True) + jnp.exp2(sink_row - m)
        o_t = lax.dot_general(vc, p_ctx.astype(BF16), (((0,), (0,)), ((), ())), preferred_element_type=F32)
        if band:
            p_band = jnp.exp2(s_band * c - m)
            denom = denom + jnp.sum(p_band, axis=0, keepdims=True)
            o_t = o_t + lax.dot_general(v_ref[pl.ds(ws, n_band), :], p_band.astype(BF16),
                                        (((0,), (0,)), ((), ())), preferred_element_type=F32)
        o_t = o_t / denom
        for g in range(GQA):
            o_ref[pl.ds(q0, WINDOW), g * DH:(g + 1) * DH] = (
                o_t[:, g * WINDOW:(g + 1) * WINDOW].T.astype(o_ref.dtype))
        return carry

    lax.fori_loop(0, q_ref.shape[0] // WINDOW, block, 0, unroll=2)


def _attention(qkv_q, qkv_kv, qkv_ctx, sink, batch, band):
    seq_q = qkv_q.shape[0] // batch
    seq_k = qkv_kv.shape[0] // batch
    n_ctx = qkv_ctx.shape[0] // batch
    k_col = N_HEADS
    v_col = N_HEADS + N_KV
    return pl.pallas_call(
        functools.partial(_attn_kernel, band=band, seq=seq_k),
        grid_spec=pltpu.PrefetchScalarGridSpec(
            num_scalar_prefetch=1,
            grid=(batch, N_KV),
            in_specs=[
                pl.BlockSpec((seq_q, GQA * DH), lambda b, k, s: (b, k)),
                pl.BlockSpec((seq_k, DH), lambda b, k, s: (b, k_col + k)),
                pl.BlockSpec((seq_k, DH), lambda b, k, s: (b, v_col + k)),
                pl.BlockSpec((n_ctx, DH), lambda b, k, s: (b, k_col + k)),
                pl.BlockSpec((n_ctx, DH), lambda b, k, s: (b, v_col + k)),
            ],
            out_specs=pl.BlockSpec((seq_q, GQA * DH), lambda b, k, s: (b, k)),
            scratch_shapes=[pltpu.VMEM((3, 3 * WINDOW, GQA * WINDOW), F32)] if band else [],
        ),
        out_shape=jax.ShapeDtypeStruct((qkv_q.shape[0], D), BF16),
        compiler_params=_params(2),
        name="window_attn" if band else "ctx_attn",
    )(sink, qkv_q, qkv_kv, qkv_kv, qkv_ctx, qkv_ctx)


def _conv_in_kernel(x_ref, wb_ref, wc_ref, wx_ref, b_ref, u_ref):
    x = x_ref[...]
    b_ref[...] = _dot(x, wb_ref[...])
    u_ref[...] = _dot(x, wc_ref[...]) * _dot(x, wx_ref[...])


def _conv_in(a, w_in):
    m = a.shape[0]
    nt = D // TN_CONV
    o_spec = pl.BlockSpec((TM_MM, TN_CONV), lambda j, i: (i, j))
    return pl.pallas_call(
        _conv_in_kernel,
        grid=(nt, m // TM_MM),
        in_specs=[
            pl.BlockSpec((TM_MM, D), lambda j, i: (i, 0)),
            pl.BlockSpec((D, TN_CONV), lambda j, i: (0, j)),
            pl.BlockSpec((D, TN_CONV), lambda j, i: (0, nt + j)),
            pl.BlockSpec((D, TN_CONV), lambda j, i: (0, 2 * nt + j)),
        ],
        out_specs=[o_spec, o_spec],
        out_shape=[jax.ShapeDtypeStruct((m, D), F32)] * 2,
        compiler_params=_params(2),
        name="conv_in_proj",
    )(a, w_in, w_in, w_in)


def _conv_gate_kernel(b_ref, u_ref, up_ref, un_ref, w_ref, o_ref, *, seq):
    i = pl.program_id(0)
    tm = u_ref.shape[0]
    u = u_ref[...]
    t0 = i * tm
    prev_row = jnp.where(lax.rem(t0, seq) == 0, 0.0, up_ref[SUBLANE - 1:SUBLANE, :])
    next_row = jnp.where(lax.rem(t0 + tm, seq) == 0, 0.0, un_ref[0:1, :])
    row = lax.broadcasted_iota(jnp.int32, u.shape, 0)
    u_prev = jnp.where(row == 0, prev_row, pltpu.roll(u, 1, axis=0))
    u_next = jnp.where(row == tm - 1, next_row, pltpu.roll(u, tm - 1, axis=0))
    y = u_prev * w_ref[0:1, :] + u * w_ref[1:2, :] + u_next * w_ref[2:3, :]
    o_ref[...] = (b_ref[...] * y).astype(o_ref.dtype)


def _conv_gate(b, u, conv_w, seq):
    m = u.shape[0]
    tm = TM_CONV
    per = tm // SUBLANE
    last = m // SUBLANE - 1
    return pl.pallas_call(
        functools.partial(_conv_gate_kernel, seq=seq),
        grid=(m // tm,),
        in_specs=[
            pl.BlockSpec((tm, D), lambda i: (i, 0)),
            pl.BlockSpec((tm, D), lambda i: (i, 0)),
            pl.BlockSpec((SUBLANE, D), lambda i: (jnp.maximum(i * per - 1, 0), 0)),
            pl.BlockSpec((SUBLANE, D), lambda i: (jnp.minimum((i + 1) * per, last), 0)),
            pl.BlockSpec((3, D), lambda i: (0, 0)),
        ],
        out_specs=pl.BlockSpec((tm, D), lambda i: (i, 0)),
        out_shape=jax.ShapeDtypeStruct((m, D), BF16),
        compiler_params=_params(1),
        name="conv_gate",
    )(b, u, u, u, conv_w)


def _route_kernel(h_ref, g_ref, sh_ref, sc_ref, wr_ref, bias_ref, f_ref, route_ref, cnt_ref, run_ref):
    i = pl.program_id(0)
    tm = h_ref.shape[0]

    @pl.when(i == 0)
    def _():
        run_ref[...] = jnp.zeros_like(run_ref)

    f = _rms(h_ref[...], g_ref[...]) * (1.0 + sc_ref[...]) + sh_ref[...]
    f_ref[...] = f
    logits = jnp.dot(f, wr_ref[...], preferred_element_type=F32, precision=lax.Precision.HIGHEST)
    lt = logits.T[:N_EXP]
    score = jax.nn.sigmoid(lt)
    sel = score + bias_ref[...]
    sel_rows = [sel[e:e + 1] for e in range(N_EXP)]
    score_rows = [score[e:e + 1] for e in range(N_EXP)]

    grp_scores = []
    for gi in range(N_GRP):
        a, b, c, d = sel_rows[gi * EXP_PER_GRP:(gi + 1) * EXP_PER_GRP]
        hi1, lo1 = jnp.maximum(a, b), jnp.minimum(a, b)
        hi2, lo2 = jnp.maximum(c, d), jnp.minimum(c, d)
        grp_scores.append(jnp.maximum(hi1, hi2) + jnp.maximum(jnp.minimum(hi1, hi2), jnp.maximum(lo1, lo2)))
    best = grp_scores[0]
    g_idx = jnp.zeros_like(best, dtype=jnp.int32)
    for gi in range(1, N_GRP):
        better = grp_scores[gi] > best
        g_idx = jnp.where(better, gi, g_idx)
        best = jnp.where(better, grp_scores[gi], best)

    masked = [jnp.where(g_idx == e // EXP_PER_GRP, sel_rows[e], NEG) for e in range(N_EXP)]
    v1 = masked[0]
    e1 = jnp.zeros_like(g_idx)
    for e in range(1, N_EXP):
        better = masked[e] > v1
        e1 = jnp.where(better, e, e1)
        v1 = jnp.where(better, masked[e], v1)
    v2 = jnp.full_like(v1, -jnp.inf)
    e2 = jnp.zeros_like(g_idx)
    for e in range(N_EXP):
        better = (masked[e] > v2) & (e1 != e)
        e2 = jnp.where(better, e, e2)
        v2 = jnp.where(better, masked[e], v2)
    s1 = jnp.zeros_like(v1)
    s2 = jnp.zeros_like(v1)
    for e in range(N_EXP):
        s1 = jnp.where(e1 == e, score_rows[e], s1)
        s2 = jnp.where(e2 == e, score_rows[e], s2)
    tot = s1 + s2
    w1 = s1 / tot
    w2 = s2 / tot

    e_iota = lax.broadcasted_iota(jnp.int32, (N_EXP, tm), 0)
    hit = (e_iota == e1) | (e_iota == e2)
    onehot = jnp.where(hit, 1.0, 0.0).astype(BF16)
    r_i = lax.broadcasted_iota(jnp.int32, (tm, tm), 0)
    c_i = lax.broadcasted_iota(jnp.int32, (tm, tm), 1)
    upper = jnp.where(r_i <= c_i, 1.0, 0.0).astype(BF16)
    prefix = _dot(onehot, upper)
    rank = prefix - 1.0 + run_ref[:, 0:1]
    rank1 = jnp.sum(jnp.where(e_iota == e1, rank, 0.0), axis=0, keepdims=True)
    rank2 = jnp.sum(jnp.where(e_iota == e2, rank, 0.0), axis=0, keepdims=True)
    run_ref[...] = run_ref[...] + jnp.sum(jnp.where(hit, 1.0, 0.0), axis=1, keepdims=True)
    cnt_ref[...] = run_ref[...]

    route_ref[0:1, :] = e1.astype(F32)
    route_ref[1:2, :] = e2.astype(F32)
    route_ref[2:3, :] = w1
    route_ref[3:4, :] = w2
    route_ref[4:5, :] = rank1
    route_ref[5:6, :] = rank2
    route_ref[6:8, :] = jnp.zeros((2, tm), F32)


def _route(h, g, mod3, layer, rows_per_cond, cond_base, wr_pad, bias_col):
    m = h.shape[0]
    tm = min(TM_ROW, rows_per_cond)

    def row(which):
        return lambda i: (_mod_row(layer, cond_base + (i * tm) // rows_per_cond, which), 0, 0)

    return pl.pallas_call(
        _route_kernel,
        grid=(m // tm,),
        in_specs=[
            pl.BlockSpec((tm, D), lambda i: (i, 0)),
            pl.BlockSpec((1, D), lambda i: (0, 0)),
            pl.BlockSpec((None, 1, D), row(3)),
            pl.BlockSpec((None, 1, D), row(4)),
            pl.BlockSpec((D, LANE), lambda i: (0, 0)),
            pl.BlockSpec((N_EXP, 1), lambda i: (0, 0)),
        ],
        out_specs=[
            pl.BlockSpec((tm, D), lambda i: (i, 0)),
            pl.BlockSpec((SUBLANE, tm), lambda i: (0, i)),
            pl.BlockSpec((N_EXP, LANE), lambda i: (0, 0)),
        ],
        out_shape=[
            jax.ShapeDtypeStruct((m, D), F32),
            jax.ShapeDtypeStruct((SUBLANE, m), F32),
            jax.ShapeDtypeStruct((N_EXP, LANE), F32),
        ],
        scratch_shapes=[pltpu.VMEM((N_EXP, LANE), F32)],
        compiler_params=_params(1),
        name="norm_route",
    )(h, g.reshape(1, D), mod3, mod3, wr_pad, bias_col)


def _row_gather(idx_ref, src_hbm, buf, sem, slot, n_rows):
    def issue(j, carry):
        for queue in range(N_DMA_QUEUES):
            k = j * N_DMA_QUEUES + queue
            pltpu.make_async_copy(src_hbm.at[pl.ds(idx_ref[0, k], 1)], buf.at[slot, pl.ds(k, 1)],
                                  sem.at[slot]).start(priority=queue)
        return carry
    lax.fori_loop(0, n_rows // N_DMA_QUEUES, issue, 0)


def _row_gather_wait(src_hbm, buf, sem, slot, n_rows):
    pltpu.make_async_copy(src_hbm.at[pl.ds(0, n_rows)], buf.at[slot], sem.at[slot]).wait()


def _dispatch_kernel(nvalid_ref, idx_ref, idx_next_ref, f_hbm, o_ref, buf, sem):
    i = pl.program_id(0)
    tm = o_ref.shape[0]
    n_valid = nvalid_ref[0]
    slot = lax.rem(i, 2)

    @pl.when((i == 0) & (n_valid > 0))
    def _():
        _row_gather(idx_ref, f_hbm, buf, sem, 0, tm)

    @pl.when(i + 1 < n_valid)
    def _():
        _row_gather(idx_next_ref, f_hbm, buf, sem, 1 - slot, tm)

    @pl.when(i < n_valid)
    def _():
        _row_gather_wait(f_hbm, buf, sem, slot, tm)
        o_ref[...] = buf[slot].astype(o_ref.dtype)

    @pl.when(i >= n_valid)
    def _():
        o_ref[...] = jnp.zeros_like(o_ref)


def _dispatch(f, src_tiles, n_valid_tiles):
    nt = src_tiles.shape[0]
    idx_spec = pl.BlockSpec((None, 1, TM_EXP), lambda i, nv: (i, 0, 0), memory_space=pltpu.SMEM)
    idx_next_spec = pl.BlockSpec((None, 1, TM_EXP), lambda i, nv: (jnp.minimum(i + 1, nt - 1), 0, 0),
                                 memory_space=pltpu.SMEM)
    return pl.pallas_call(
        _dispatch_kernel,
        grid_spec=pltpu.PrefetchScalarGridSpec(
            num_scalar_prefetch=1,
            grid=(nt,),
            in_specs=[idx_spec, idx_next_spec, pl.BlockSpec(memory_space=pl.ANY)],
            out_specs=pl.BlockSpec((TM_EXP, D), lambda i, nv: (i, 0)),
            scratch_shapes=[pltpu.VMEM((2, TM_EXP, D), F32), pltpu.SemaphoreType.DMA((2,))],
        ),
        out_shape=jax.ShapeDtypeStruct((nt * TM_EXP, D), BF16),
        compiler_params=_params(1),
        name="moe_dispatch",
    )(n_valid_tiles, src_tiles, src_tiles, f)


(ST_EXPERT, ST_WCOL, ST_XROW, ST_OROW, ST_OCOL, ST_VALID, ST_FIRST, ST_SLOT, ST_NEXT_E, ST_NEXT_WCOL,
 ST_HAS_NEXT) = range(11)


def _gate_up_weights(w_hbm, wbuf, sem, layer, e, wcol, slot):
    nf = D_FF // TF_EXP
    col_g = pl.multiple_of(wcol * TF_EXP, TF_EXP)
    col_u = pl.multiple_of((nf + wcol) * TF_EXP, TF_EXP)
    return (pltpu.make_async_copy(w_hbm.at[layer, e, :, pl.ds(col_g, TF_EXP)], wbuf.at[slot, 0], sem.at[slot, 0]),
            pltpu.make_async_copy(w_hbm.at[layer, e, :, pl.ds(col_u, TF_EXP)], wbuf.at[slot, 1], sem.at[slot, 1]))


def _gate_up_kernel(st_ref, x_ref, w_hbm, o_ref, wbuf, sem, *, layer):
    s = pl.program_id(0)
    valid = st_ref[ST_VALID, s] > 0
    slot = st_ref[ST_SLOT, s]

    @pl.when(valid & (st_ref[ST_FIRST, s] > 0))
    def _():
        cur = _gate_up_weights(w_hbm, wbuf, sem, layer, st_ref[ST_EXPERT, s], st_ref[ST_WCOL, s], slot)

        @pl.when(s == 0)
        def _():
            for c in cur:
                c.start(priority=WEIGHT_QUEUE)

        for c in cur:
            c.wait()

        @pl.when(st_ref[ST_HAS_NEXT, s] > 0)
        def _():
            for c in _gate_up_weights(w_hbm, wbuf, sem, layer, st_ref[ST_NEXT_E, s], st_ref[ST_NEXT_WCOL, s],
                                      1 - slot):
                c.start(priority=WEIGHT_QUEUE)

    @pl.when(valid)
    def _():
        x = x_ref[...]
        gate = _dot(x, wbuf[slot, 0])
        up = _dot(x, wbuf[slot, 1])
        o_ref[...] = (gate * jax.nn.sigmoid(gate) * up).astype(o_ref.dtype)

    @pl.when(jnp.logical_not(valid))
    def _():
        o_ref[...] = jnp.zeros_like(o_ref)


def _gate_up(xs, w_gate_up, layer, steps):
    n_steps = steps.shape[1]
    return pl.pallas_call(
        functools.partial(_gate_up_kernel, layer=layer),
        grid_spec=pltpu.PrefetchScalarGridSpec(
            num_scalar_prefetch=1,
            grid=(n_steps,),
            in_specs=[
                pl.BlockSpec((TM_EXP, D), lambda s, st: (st[ST_XROW, s], 0)),
                pl.BlockSpec(memory_space=pl.ANY),
            ],
            out_specs=pl.BlockSpec((TM_EXP, TF_EXP), lambda s, st: (st[ST_OROW, s], st[ST_OCOL, s])),
            scratch_shapes=[pltpu.VMEM((2, 2, D, TF_EXP), F32), pltpu.SemaphoreType.DMA((2, 2))],
        ),
        out_shape=jax.ShapeDtypeStruct((xs.shape[0], D_FF), BF16),
        compiler_params=_params(1),
        name="moe_gate_up",
    )(steps, xs, w_gate_up)


TL_EXPERT, TL_SRC, TL_FIRST, TL_SLOT, TL_NEXT_E, TL_HAS_NEXT = range(6)


def _down_weights(w_hbm, wbuf, sem, layer, e, slot):
    return pltpu.make_async_copy(w_hbm.at[layer, e], wbuf.at[slot], sem.at[slot])


def _down_kernel(tl_ref, a_ref, w_hbm, o_ref, wbuf, sem, *, layer):
    i = pl.program_id(0)
    valid = tl_ref[TL_SRC, i] == i
    slot = tl_ref[TL_SLOT, i]

    @pl.when(valid & (tl_ref[TL_FIRST, i] > 0))
    def _():
        cur = _down_weights(w_hbm, wbuf, sem, layer, tl_ref[TL_EXPERT, i], slot)

        @pl.when(i == 0)
        def _():
            cur.start(priority=WEIGHT_QUEUE)

        cur.wait()

        @pl.when(tl_ref[TL_HAS_NEXT, i] > 0)
        def _():
            _down_weights(w_hbm, wbuf, sem, layer, tl_ref[TL_NEXT_E, i], 1 - slot).start(priority=WEIGHT_QUEUE)

    @pl.when(valid)
    def _():
        o_ref[...] = _dot(a_ref[...], wbuf[slot])

    @pl.when(jnp.logical_not(valid))
    def _():
        o_ref[...] = jnp.zeros_like(o_ref)


def _down(act, w_down, layer, tiles):
    nt = tiles.shape[1]
    return pl.pallas_call(
        functools.partial(_down_kernel, layer=layer),
        grid_spec=pltpu.PrefetchScalarGridSpec(
            num_scalar_prefetch=1,
            grid=(nt,),
            in_specs=[
                pl.BlockSpec((TM_EXP, D_FF), lambda i, tl: (tl[TL_SRC, i], 0)),
                pl.BlockSpec(memory_space=pl.ANY),
            ],
            out_specs=pl.BlockSpec((TM_EXP, D), lambda i, tl: (i, 0)),
            scratch_shapes=[pltpu.VMEM((2, D_FF, D), F32), pltpu.SemaphoreType.DMA((2,))],
        ),
        out_shape=jax.ShapeDtypeStruct((act.shape[0], D), F32),
        compiler_params=_params(1),
        name="moe_down",
    )(tiles, act, w_down)


def _combine_kernel(d1_ref, d2_ref, d1n_ref, d2n_ref, y_hbm, h_ref, w1_ref, w2_ref, gate_ref, fg_ref, o_ref,
                    buf1, buf2, sem1, sem2, *, final_norm):
    i = pl.program_id(0)
    n = pl.num_programs(0)
    tc = o_ref.shape[0]
    slot = lax.rem(i, 2)

    @pl.when(i == 0)
    def _():
        _row_gather(d1_ref, y_hbm, buf1, sem1, 0, tc)
        _row_gather(d2_ref, y_hbm, buf2, sem2, 0, tc)

    @pl.when(i + 1 < n)
    def _():
        _row_gather(d1n_ref, y_hbm, buf1, sem1, 1 - slot, tc)
        _row_gather(d2n_ref, y_hbm, buf2, sem2, 1 - slot, tc)

    _row_gather_wait(y_hbm, buf1, sem1, slot, tc)
    _row_gather_wait(y_hbm, buf2, sem2, slot, tc)
    moe = w1_ref[:, 0:1] * buf1[slot] + w2_ref[:, 0:1] * buf2[slot]
    out = h_ref[...] + gate_ref[...] * moe
    if final_norm:
        out = _rms(out, fg_ref[...])
    o_ref[...] = out


def _combine(y, d1_tiles, d2_tiles, h, w1b, w2b, mod3, layer, rows_per_cond, cond_base, final_g):
    m = h.shape[0]
    tc = TC_COMB
    nt = m // tc
    cur = pl.BlockSpec((None, 1, tc), lambda i: (i, 0, 0), memory_space=pltpu.SMEM)
    nxt = pl.BlockSpec((None, 1, tc), lambda i: (jnp.minimum(i + 1, nt - 1), 0, 0), memory_space=pltpu.SMEM)
    fg = jnp.ones((1, D), F32) if final_g is None else final_g.reshape(1, D)
    return pl.pallas_call(
        functools.partial(_combine_kernel, final_norm=final_g is not None),
        grid=(nt,),
        in_specs=[
            cur, cur, nxt, nxt,
            pl.BlockSpec(memory_space=pl.ANY),
            pl.BlockSpec((tc, D), lambda i: (i, 0)),
            pl.BlockSpec((tc, LANE), lambda i: (i, 0)),
            pl.BlockSpec((tc, LANE), lambda i: (i, 0)),
            pl.BlockSpec((None, 1, D), lambda i: (_mod_row(layer, cond_base + (i * tc) // rows_per_cond, 5), 0, 0)),
            pl.BlockSpec((1, D), lambda i: (0, 0)),
        ],
        out_specs=pl.BlockSpec((tc, D), lambda i: (i, 0)),
        out_shape=jax.ShapeDtypeStruct((m, D), F32),
        scratch_shapes=[pltpu.VMEM((2, tc, D), F32), pltpu.VMEM((2, tc, D), F32),
                        pltpu.SemaphoreType.DMA((2,)), pltpu.SemaphoreType.DMA((2,))],
        compiler_params=_params(1),
        name="moe_combine",
    )(d1_tiles, d2_tiles, d1_tiles, d2_tiles, y, h, w1b, w2b, mod3, fg)


def _moe_block(h, norm_g, mod3, layer, rows_per_cond, cond_base, wr_pad, bias_col, w_gate_up, w_down, final_g):
    t = h.shape[0]
    f, route, cnt = _route(h, norm_g, mod3, layer, rows_per_cond, cond_base, wr_pad, bias_col)

    i32 = jnp.int32
    e_ids = jnp.arange(N_EXP, dtype=i32)

    def take(table, idx):
        return jnp.sum(jnp.where(idx[:, None] == e_ids[None, :], table[None, :], 0), axis=1).astype(i32)

    def bucket(ends, pos):
        return jnp.sum((ends[None, :] <= pos[:, None]).astype(i32), axis=1)

    e1 = route[0].astype(i32)
    e2 = route[1].astype(i32)
    counts = cnt[:, 0].astype(i32)
    tiles_per_e = (counts + TM_EXP - 1) // TM_EXP
    tile_end = jnp.cumsum(tiles_per_e).astype(i32)
    tile_start = tile_end - tiles_per_e
    row_start = tile_start * TM_EXP
    d1 = take(row_start, e1) + route[4].astype(i32)
    d2 = take(row_start, e2) + route[5].astype(i32)
    nt = 2 * t // TM_EXP + N_EXP
    tok = jnp.arange(t, dtype=i32)
    src = jnp.zeros((nt * TM_EXP,), i32).at[jnp.concatenate([d1, d2])].set(jnp.concatenate([tok, tok]))
    n_valid = jnp.maximum(tile_end[-1], 1)

    nonempty = tiles_per_e > 0
    order = (jnp.cumsum(nonempty.astype(i32)) - nonempty.astype(i32)).astype(i32)
    later = nonempty[None, :] & (e_ids[None, :] > e_ids[:, None])
    next_e = jnp.min(jnp.where(later, e_ids[None, :], N_EXP), axis=1).astype(i32)

    tile_id = jnp.arange(nt, dtype=i32)
    tile_src = jnp.minimum(tile_id, n_valid - 1)
    tile_e = jnp.minimum(bucket(tile_end, tile_src), N_EXP - 1)
    tile_next = take(next_e, tile_e)
    tiles = jnp.stack([
        tile_e, tile_src,
        (tile_src == take(tile_start, tile_e)).astype(i32),
        take(order, tile_e) % 2,
        jnp.minimum(tile_next, N_EXP - 1),
        (tile_next < N_EXP).astype(i32),
    ]).astype(i32)

    nf = D_FF // TF_EXP
    step_id = jnp.arange(nf * nt, dtype=i32)
    step_valid = step_id < nf * n_valid
    sid = jnp.minimum(step_id, nf * n_valid - 1)
    step_e = jnp.minimum(bucket(nf * tile_end, sid), N_EXP - 1)
    local = sid - nf * take(tile_start, step_e)
    n_e = jnp.maximum(take(tiles_per_e, step_e), 1)
    step_f = local // n_e
    step_r = take(tile_start, step_e) + local % n_e
    pad = step_id - nf * n_valid
    last_col = step_f == nf - 1
    step_next_e = jnp.where(last_col, take(next_e, step_e), step_e)
    steps = jnp.stack([
        step_e, step_f, step_r,
        jnp.where(step_valid, step_r, n_valid + pad // nf),
        jnp.where(step_valid, step_f, pad % nf),
        step_valid.astype(i32),
        (local % n_e == 0).astype(i32),
        (nf * take(order, step_e) + step_f) % 2,
        jnp.minimum(step_next_e, N_EXP - 1),
        jnp.where(last_col, 0, step_f + 1),
        (step_next_e < N_EXP).astype(i32),
    ]).astype(i32)

    xs = _dispatch(f, src.reshape(nt, 1, TM_EXP), n_valid.reshape(1).astype(i32))
    act = _gate_up(xs, w_gate_up, layer, steps)
    y = _down(act, w_down, layer, tiles)

    ntc = t // TC_COMB
    w1b = jnp.broadcast_to(route[2][:, None], (t, LANE))
    w2b = jnp.broadcast_to(route[3][:, None], (t, LANE))
    return _combine(y, d1.reshape(ntc, 1, TC_COMB), d2.reshape(ntc, 1, TC_COMB), h, w1b, w2b,
                    mod3, layer, rows_per_cond, cond_base, final_g)


def _rope_tables(seq):
    rows = seq // GRID_W
    row = jnp.repeat(jnp.arange(rows), GRID_W).astype(F32)
    col = jnp.tile(jnp.arange(GRID_W), rows).astype(F32)
    n_freq = DH // 4
    inv_freq = ROPE_THETA ** (-jnp.arange(n_freq, dtype=F32) / n_freq)
    ang = jnp.concatenate([row[:, None] * inv_freq, col[:, None] * inv_freq], axis=-1)
    cos, sin = jnp.cos(ang), jnp.sin(ang)
    return jnp.concatenate([cos, cos], axis=-1), jnp.concatenate([-sin, sin], axis=-1)


def kernel(x, c, ctx, c_ctx, w_ada, b_ada, norm1_g, norm2_g, attn_w_qkv, attn_w_o, attn_sink, conv_w_in, conv_w,
           conv_w_out, w_router, router_bias, moe_w_gate_up, moe_w_down, final_g):
    batch, seq, _ = x.shape
    n_ctx = ctx.shape[1]
    depth = w_ada.shape[0]
    assert depth == 2 and x.shape[2] == D
    ctx_row = batch

    c8 = jnp.concatenate([c, c_ctx[None, :], jnp.zeros((SUBLANE - batch - 1, D), F32)], axis=0)
    mod3 = _ada(c8, w_ada, b_ada).reshape(depth * SUBLANE * N_MOD, 1, D)
    wr_pad = jnp.pad(w_router, ((0, 0), (0, LANE - N_EXP)))
    bias_col = router_bias.astype(F32).reshape(N_EXP, 1)
    rope = _rope_tables(seq)

    h_lat = x.reshape(batch * seq, D)
    h_ctx = ctx.reshape(batch * n_ctx, D)

    a_lat = _normmod(h_lat, norm1_g[0], mod3, 0, 0, seq, 0)
    a_ctx = _normmod(h_ctx, norm1_g[0], mod3, 0, 0, batch * n_ctx, ctx_row)
    qkv_lat = _qkv_proj(a_lat, attn_w_qkv[0], rope)
    qkv_ctx = _qkv_proj(a_ctx, attn_w_qkv[0])
    o_lat = _attention(qkv_lat, qkv_lat, qkv_ctx, attn_sink[0], batch, band=True)
    o_ctx = _attention(qkv_ctx, qkv_ctx, qkv_ctx, attn_sink[0], batch, band=False)
    h_lat = _proj_resid(o_lat, attn_w_o[0], h_lat, mod3, 0, 2, seq, 0)
    h_ctx = _proj_resid(o_ctx, attn_w_o[0], h_ctx, mod3, 0, 2, batch * n_ctx, ctx_row)
    moe_args = (wr_pad, bias_col, moe_w_gate_up, moe_w_down)
    h_lat = _moe_block(h_lat, norm2_g[0], mod3, 0, seq, 0, *moe_args, None)
    h_ctx = _moe_block(h_ctx, norm2_g[0], mod3, 0, batch * n_ctx, ctx_row, *moe_args, None)

    a_lat = _normmod(h_lat, norm1_g[1], mod3, 1, 0, seq, 0)
    b_gate, u = _conv_in(a_lat, conv_w_in[0])
    z = _conv_gate(b_gate, u, conv_w[0], seq)
    h_lat = _proj_resid(z, conv_w_out[0], h_lat, mod3, 1, 2, seq, 0)
    h_lat = _moe_block(h_lat, norm2_g[1], mod3, 1, seq, 0, *moe_args, final_g)
    return h_lat.reshape(batch, seq, D)
```

```python
import functools

import jax
import jax.numpy as jnp
from jax import lax
from jax.experimental import pallas as pl
from jax.experimental.pallas import tpu as pltpu

D = 4096
N_HEADS = 32
N_KV = 8
GQA = 4
DH = 128
QKV = (N_HEADS + 2 * N_KV) * DH
GRID_W = 64
WINDOW = 128
ROPE_THETA = 10000.0
N_EXP = 16
N_GRP = 4
EXP_PER_GRP = 4
D_FF = 1024
N_MOD = 6
EPS = 1e-6
NEG = -1e30

LANE = 128
SUBLANE = 8
BF16_ROWS = 2 * SUBLANE
VMEM_LIMIT = 56 * 1024 * 1024
N_DMA_QUEUES = 2
WEIGHT_QUEUE = 1

TM_MM = 512
TN_MM = 1024
TN_CONV = 256
TM_CONV_IN = 1024
TM_ROW = 512
TM_CONV = 256
TM_EXP = 256
TF_EXP = 512
TC_COMB = 256

F32 = jnp.float32
BF16 = jnp.bfloat16


def _params(n_axes):
    return pltpu.CompilerParams(dimension_semantics=("arbitrary",) * n_axes,
                                vmem_limit_bytes=VMEM_LIMIT)


def _dot(a, b):
    return lax.dot_general(a, b, (((1,), (0,)), ((), ())), preferred_element_type=F32)


def _dot_nt(a, b):
    return lax.dot_general(a, b, (((1,), (1,)), ((), ())), preferred_element_type=F32)


def _ada_kernel(c_ref, w_ref, b_ref, o_ref):
    s = c_ref[...]
    s = (s * jax.nn.sigmoid(s)).astype(BF16)
    o_ref[...] = _dot(s, w_ref[...]) + b_ref[...]


def _ada(c8, w_ada, b_ada):
    depth = w_ada.shape[0]
    n = w_ada.shape[2]
    return pl.pallas_call(
        _ada_kernel,
        grid=(depth, n // TN_MM),
        in_specs=[
            pl.BlockSpec((SUBLANE, D), lambda l, j: (0, 0)),
            pl.BlockSpec((None, D, TN_MM), lambda l, j: (l, 0, j)),
            pl.BlockSpec((None, 1, TN_MM), lambda l, j: (l, 0, j)),
        ],
        out_specs=pl.BlockSpec((None, SUBLANE, TN_MM), lambda l, j: (l, 0, j)),
        out_shape=jax.ShapeDtypeStruct((depth, SUBLANE, n), F32),
        compiler_params=_params(2),
        name="ada_mod",
    )(c8, w_ada, b_ada.reshape(depth, 1, n))


def _mod_row(layer, cond_row, which):
    return (layer * SUBLANE + cond_row) * N_MOD + which


def _rms(x, g):
    return x * lax.rsqrt(jnp.mean(x * x, axis=-1, keepdims=True) + EPS) * g


def _normmod_kernel(h_ref, g_ref, sh_ref, sc_ref, o_ref):
    y = _rms(h_ref[...], g_ref[...])
    o_ref[...] = (y * (1.0 + sc_ref[...]) + sh_ref[...]).astype(o_ref.dtype)


def _normmod(h, g, mod3, layer, which_shift, rows_per_cond, cond_base):
    m = h.shape[0]
    tm = min(TM_ROW, rows_per_cond)

    def row(which):
        return lambda i: (_mod_row(layer, cond_base + (i * tm) // rows_per_cond, which), 0, 0)

    return pl.pallas_call(
        _normmod_kernel,
        grid=(m // tm,),
        in_specs=[
            pl.BlockSpec((tm, D), lambda i: (i, 0)),
            pl.BlockSpec((1, D), lambda i: (0, 0)),
            pl.BlockSpec((None, 1, D), row(which_shift)),
            pl.BlockSpec((None, 1, D), row(which_shift + 1)),
        ],
        out_specs=pl.BlockSpec((tm, D), lambda i: (i, 0)),
        out_shape=jax.ShapeDtypeStruct((m, D), BF16),
        compiler_params=_params(1),
        name="norm_modulate",
    )(h, g.reshape(1, D), mod3, mod3)


def _mm_plain_kernel(x_ref, w_ref, o_ref):
    o_ref[...] = _dot(x_ref[...], w_ref[...]).astype(o_ref.dtype)


def _mm_rope_kernel(x_ref, w_ref, cos_ref, sin_ref, o_ref, *, n_rope_tiles):
    acc = _dot(x_ref[...], w_ref[...])

    @pl.when(pl.program_id(0) < n_rope_tiles)
    def _():
        cosf = cos_ref[...]
        sinf = sin_ref[...]
        for hh in range(acc.shape[1] // DH):
            xh = acc[:, hh * DH:(hh + 1) * DH]
            o_ref[:, hh * DH:(hh + 1) * DH] = (
                xh * cosf + pltpu.roll(xh, DH // 2, axis=1) * sinf).astype(o_ref.dtype)

    @pl.when(pl.program_id(0) >= n_rope_tiles)
    def _():
        o_ref[...] = acc.astype(o_ref.dtype)


def _mm_resid_kernel(x_ref, w_ref, h_ref, gate_ref, o_ref):
    o_ref[...] = h_ref[...] + gate_ref[...] * _dot(x_ref[...], w_ref[...])


def _qkv_proj(a, w_qkv, rope=None):
    m = a.shape[0]
    grid = (QKV // TN_MM, m // TM_MM)
    x_spec = pl.BlockSpec((TM_MM, D), lambda j, i: (i, 0))
    w_spec = pl.BlockSpec((D, TN_MM), lambda j, i: (0, j))
    o_spec = pl.BlockSpec((TM_MM, TN_MM), lambda j, i: (i, j))
    out_shape = jax.ShapeDtypeStruct((m, QKV), BF16)
    if rope is None:
        return pl.pallas_call(_mm_plain_kernel, grid=grid, in_specs=[x_spec, w_spec], out_specs=o_spec,
                              out_shape=out_shape, compiler_params=_params(2), name="qkv_ctx")(a, w_qkv)
    cosf, sinf = rope
    seq_tiles = cosf.shape[0] // TM_MM
    t_spec = pl.BlockSpec((TM_MM, DH), lambda j, i: (i % seq_tiles, 0))
    n_rope_tiles = (N_HEADS + N_KV) * DH // TN_MM
    return pl.pallas_call(
        functools.partial(_mm_rope_kernel, n_rope_tiles=n_rope_tiles),
        grid=grid, in_specs=[x_spec, w_spec, t_spec, t_spec], out_specs=o_spec,
        out_shape=out_shape, compiler_params=_params(2), name="qkv_rope")(a, w_qkv, cosf, sinf)


def _proj_resid(x, w, h, mod3, layer, which_gate, rows_per_cond, cond_base):
    m = x.shape[0]
    tm = min(TM_MM, rows_per_cond)

    def gate_map(j, i):
        return (_mod_row(layer, cond_base + (i * tm) // rows_per_cond, which_gate), 0, j)

    return pl.pallas_call(
        _mm_resid_kernel,
        grid=(D // TN_MM, m // tm),
        in_specs=[
            pl.BlockSpec((tm, D), lambda j, i: (i, 0)),
            pl.BlockSpec((D, TN_MM), lambda j, i: (0, j)),
            pl.BlockSpec((tm, TN_MM), lambda j, i: (i, j)),
            pl.BlockSpec((None, 1, TN_MM), gate_map),
        ],
        out_specs=pl.BlockSpec((tm, TN_MM), lambda j, i: (i, j)),
        out_shape=jax.ShapeDtypeStruct((m, D), F32),
        compiler_params=_params(2),
        name="proj_residual",
    )(x, w, h, mod3)


def _attn_kernel(sink_ref, q_ref, k_ref, v_ref, kc_ref, vc_ref, o_ref, *scratch, band, seq):
    kh = pl.program_id(1)
    log2e = 1.4426950408889634
    c = DH ** -0.5 * log2e
    kc = kc_ref[...]
    vc = vc_ref[...]
    cols = GQA * WINDOW
    lane = lax.broadcasted_iota(jnp.int32, (1, cols), 1)
    sink_row = jnp.full((1, cols), sink_ref[kh * GQA + GQA - 1], F32)
    for g in range(GQA - 2, -1, -1):
        sink_row = jnp.where(lane < (g + 1) * WINDOW, sink_ref[kh * GQA + g], sink_row)
    sink_row = sink_row * log2e
    n_band = 3 * WINDOW if band else 0
    if band:
        (mask_ref,) = scratch
        key_minus_query = (lax.broadcasted_iota(jnp.int32, (n_band, cols), 0)
                           - (lax.broadcasted_iota(jnp.int32, (n_band, cols), 1) & (WINDOW - 1)))
        for t in range(3):
            valid = (key_minus_query >= (t - 1) * WINDOW) & (key_minus_query <= (t + 1) * WINDOW)
            mask_ref[t] = jnp.where(valid, 0.0, NEG)

    def block(n, carry):
        q0 = pl.multiple_of(n * WINDOW, WINDOW)
        qb = q_ref[pl.ds(q0, WINDOW), :]
        q4 = jnp.concatenate([qb[:, g * DH:(g + 1) * DH] for g in range(GQA)], axis=0)
        s_ctx = _dot_nt(kc, q4)
        m_raw = jnp.max(s_ctx, axis=0, keepdims=True)
        if band:
            ws = pl.multiple_of(jnp.clip((n - 1) * WINDOW, 0, seq - n_band), WINDOW)
            s_band = _dot_nt(k_ref[pl.ds(ws, n_band), :], q4) + mask_ref[(q0 - ws) // WINDOW]
            m_raw = jnp.maximum(m_raw, jnp.max(s_band, axis=0, keepdims=True))
        m = jnp.maximum(m_raw * c, sink_row)
        p_ctx = jnp.exp2(s_ctx * c - m)
        denom = jnp.sum(p_ctx, axis=0, keepdims=True) + jnp.exp2(sink_row - m)
        o_t = lax.dot_general(vc, p_ctx.astype(BF16), (((0,), (0,)), ((), ())), preferred_element_type=F32)
        if band:
            p_band = jnp.exp2(s_band * c - m)
            denom = denom + jnp.sum(p_band, axis=0, keepdims=True)
            o_t = o_t + lax.dot_general(v_ref[pl.ds(ws, n_band), :], p_band.astype(BF16),
                                        (((0,), (0,)), ((), ())), preferred_element_type=F32)
        o_t = o_t / denom
        for g in range(GQA):
            o_ref[pl.ds(q0, WINDOW), g * DH:(g + 1) * DH] = (
                o_t[:, g * WINDOW:(g + 1) * WINDOW].T.astype(o_ref.dtype))
        return carry

    lax.fori_loop(0, q_ref.shape[0] // WINDOW, block, 0, unroll=2)


def _attention(qkv_q, qkv_kv, qkv_ctx, sink, batch, band):
    seq_q = qkv_q.shape[0] // batch
    seq_k = qkv_kv.shape[0] // batch
    n_ctx = qkv_ctx.shape[0] // batch
    k_col = N_HEADS
    v_col = N_HEADS + N_KV
    return pl.pallas_call(
        functools.partial(_attn_kernel, band=band, seq=seq_k),
        grid_spec=pltpu.PrefetchScalarGridSpec(
            num_scalar_prefetch=1,
            grid=(batch, N_KV),
            in_specs=[
                pl.BlockSpec((seq_q, GQA * DH), lambda b, k, s: (b, k)),
                pl.BlockSpec((seq_k, DH), lambda b, k, s: (b, k_col + k)),
                pl.BlockSpec((seq_k, DH), lambda b, k, s: (b, v_col + k)),
                pl.BlockSpec((n_ctx, DH), lambda b, k, s: (b, k_col + k)),
                pl.BlockSpec((n_ctx, DH), lambda b, k, s: (b, v_col + k)),
            ],
            out_specs=pl.BlockSpec((seq_q, GQA * DH), lambda b, k, s: (b, k)),
            scratch_shapes=[pltpu.VMEM((3, 3 * WINDOW, GQA * WINDOW), F32)] if band else [],
        ),
        out_shape=jax.ShapeDtypeStruct((qkv_q.shape[0], D), BF16),
        compiler_params=_params(2),
        name="window_attn" if band else "ctx_attn",
    )(sink, qkv_q, qkv_kv, qkv_kv, qkv_ctx, qkv_ctx)


def _conv_in_kernel(x_ref, wb_ref, wc_ref, wx_ref, b_ref, u_ref):
    x = x_ref[...]
    b_ref[...] = _dot(x, wb_ref[...]).astype(b_ref.dtype)
    u_ref[...] = (_dot(x, wc_ref[...]) * _dot(x, wx_ref[...])).astype(u_ref.dtype)


def _conv_in(a, w_in):
    m = a.shape[0]
    nt = D // TN_CONV
    o_spec = pl.BlockSpec((TM_CONV_IN, TN_CONV), lambda j, i: (i, j))
    return pl.pallas_call(
        _conv_in_kernel,
        grid=(nt, m // TM_CONV_IN),
        in_specs=[
            pl.BlockSpec((TM_CONV_IN, D), lambda j, i: (i, 0)),
            pl.BlockSpec((D, TN_CONV), lambda j, i: (0, j)),
            pl.BlockSpec((D, TN_CONV), lambda j, i: (0, nt + j)),
            pl.BlockSpec((D, TN_CONV), lambda j, i: (0, 2 * nt + j)),
        ],
        out_specs=[o_spec, o_spec],
        out_shape=[jax.ShapeDtypeStruct((m, D), BF16)] * 2,
        compiler_params=_params(2),
        name="conv_in_proj",
    )(a, w_in, w_in, w_in)


def _conv_gate_kernel(b_ref, u_ref, up_ref, un_ref, w_ref, o_ref, *, seq):
    i = pl.program_id(0)
    tm = u_ref.shape[0]
    u = u_ref[...].astype(F32)
    t0 = i * tm
    prev_row = jnp.where(lax.rem(t0, seq) == 0, 0.0, up_ref[...].astype(F32)[BF16_ROWS - 1:BF16_ROWS, :])
    next_row = jnp.where(lax.rem(t0 + tm, seq) == 0, 0.0, un_ref[...].astype(F32)[0:1, :])
    row = lax.broadcasted_iota(jnp.int32, u.shape, 0)
    u_prev = jnp.where(row == 0, prev_row, pltpu.roll(u, 1, axis=0))
    u_next = jnp.where(row == tm - 1, next_row, pltpu.roll(u, tm - 1, axis=0))
    y = u_prev * w_ref[0:1, :] + u * w_ref[1:2, :] + u_next * w_ref[2:3, :]
    o_ref[...] = (b_ref[...].astype(F32) * y).astype(o_ref.dtype)


def _conv_gate(b, u, conv_w, seq):
    m = u.shape[0]
    tm = TM_CONV
    per = tm // BF16_ROWS
    last = m // BF16_ROWS - 1
    return pl.pallas_call(
        functools.partial(_conv_gate_kernel, seq=seq),
        grid=(m // tm,),
        in_specs=[
            pl.BlockSpec((tm, D), lambda i: (i, 0)),
            pl.BlockSpec((tm, D), lambda i: (i, 0)),
            pl.BlockSpec((BF16_ROWS, D), lambda i: (jnp.maximum(i * per - 1, 0), 0)),
            pl.BlockSpec((BF16_ROWS, D), lambda i: (jnp.minimum((i + 1) * per, last), 0)),
            pl.BlockSpec((3, D), lambda i: (0, 0)),
        ],
        out_specs=pl.BlockSpec((tm, D), lambda i: (i, 0)),
        out_shape=jax.ShapeDtypeStruct((m, D), BF16),
        compiler_params=_params(1),
        name="conv_gate",
    )(b, u, u, u, conv_w)


def _route_kernel(h_ref, g_ref, sh_ref, sc_ref, wr_ref, bias_ref, f_ref, route_ref, cnt_ref, run_ref):
    i = pl.program_id(0)
    tm = h_ref.shape[0]

    @pl.when(i == 0)
    def _():
        run_ref[...] = jnp.zeros_like(run_ref)

    f = _rms(h_ref[...], g_ref[...]) * (1.0 + sc_ref[...]) + sh_ref[...]
    f_ref[...] = f
    f_hi = f.astype(BF16)
    f_lo = (f - f_hi.astype(F32)).astype(BF16)
    logits = _dot(f_hi, wr_ref[0]) + (_dot(f_lo, wr_ref[0]) + _dot(f_hi, wr_ref[1]))
    lt = logits.T[:N_EXP]
    score = jax.nn.sigmoid(lt)
    sel = score + bias_ref[...]
    sel_rows = [sel[e:e + 1] for e in range(N_EXP)]
    score_rows = [score[e:e + 1] for e in range(N_EXP)]

    grp_scores = []
    for gi in range(N_GRP):
        a, b, c, d = sel_rows[gi * EXP_PER_GRP:(gi + 1) * EXP_PER_GRP]
        hi1, lo1 = jnp.maximum(a, b), jnp.minimum(a, b)
        hi2, lo2 = jnp.maximum(c, d), jnp.minimum(c, d)
        grp_scores.append(jnp.maximum(hi1, hi2) + jnp.maximum(jnp.minimum(hi1, hi2), jnp.maximum(lo1, lo2)))
    best = grp_scores[0]
    g_idx = jnp.zeros_like(best, dtype=jnp.int32)
    for gi in range(1, N_GRP):
        better = grp_scores[gi] > best
        g_idx = jnp.where(better, gi, g_idx)
        best = jnp.where(better, grp_scores[gi], best)

    masked = [jnp.where(g_idx == e // EXP_PER_GRP, sel_rows[e], NEG) for e in range(N_EXP)]
    v1 = masked[0]
    e1 = jnp.zeros_like(g_idx)
    for e in range(1, N_EXP):
        better = masked[e] > v1
        e1 = jnp.where(better, e, e1)
        v1 = jnp.where(better, masked[e], v1)
    v2 = jnp.full_like(v1, -jnp.inf)
    e2 = jnp.zeros_like(g_idx)
    for e in range(N_EXP):
        better = (masked[e] > v2) & (e1 != e)
        e2 = jnp.where(better, e, e2)
        v2 = jnp.where(better, masked[e], v2)
    s1 = jnp.zeros_like(v1)
    s2 = jnp.zeros_like(v1)
    for e in range(N_EXP):
        s1 = jnp.where(e1 == e, score_rows[e], s1)
        s2 = jnp.where(e2 == e, score_rows[e], s2)
    tot = s1 + s2
    w1 = s1 / tot
    w2 = s2 / tot

    e_iota = lax.broadcasted_iota(jnp.int32, (N_EXP, tm), 0)
    hit = (e_iota == e1) | (e_iota == e2)
    onehot = jnp.where(hit, 1.0, 0.0).astype(BF16)
    r_i = lax.broadcasted_iota(jnp.int32, (tm, tm), 0)
    c_i = lax.broadcasted_iota(jnp.int32, (tm, tm), 1)
    upper = jnp.where(r_i <= c_i, 1.0, 0.0).astype(BF16)
    prefix = _dot(onehot, upper)
    rank = prefix - 1.0 + run_ref[:, 0:1]
    rank1 = jnp.sum(jnp.where(e_iota == e1, rank, 0.0), axis=0, keepdims=True)
    rank2 = jnp.sum(jnp.where(e_iota == e2, rank, 0.0), axis=0, keepdims=True)
    run_ref[...] = run_ref[...] + jnp.sum(jnp.where(hit, 1.0, 0.0), axis=1, keepdims=True)
    cnt_ref[...] = run_ref[...]

    route_ref[0:1, :] = e1.astype(F32)
    route_ref[1:2, :] = e2.astype(F32)
    route_ref[2:3, :] = w1
    route_ref[3:4, :] = w2
    route_ref[4:5, :] = rank1
    route_ref[5:6, :] = rank2
    route_ref[6:8, :] = jnp.zeros((2, tm), F32)


def _route(h, g, mod3, layer, rows_per_cond, cond_base, wr_pad, bias_col):
    m = h.shape[0]
    tm = min(TM_ROW, rows_per_cond)

    def row(which):
        return lambda i: (_mod_row(layer, cond_base + (i * tm) // rows_per_cond, which), 0, 0)

    return pl.pallas_call(
        _route_kernel,
        grid=(m // tm,),
        in_specs=[
            pl.BlockSpec((tm, D), lambda i: (i, 0)),
            pl.BlockSpec((1, D), lambda i: (0, 0)),
            pl.BlockSpec((None, 1, D), row(3)),
            pl.BlockSpec((None, 1, D), row(4)),
            pl.BlockSpec((2, D, LANE), lambda i: (0, 0, 0)),
            pl.BlockSpec((N_EXP, 1), lambda i: (0, 0)),
        ],
        out_specs=[
            pl.BlockSpec((tm, D), lambda i: (i, 0)),
            pl.BlockSpec((SUBLANE, tm), lambda i: (0, i)),
            pl.BlockSpec((N_EXP, LANE), lambda i: (0, 0)),
        ],
        out_shape=[
            jax.ShapeDtypeStruct((m, D), F32),
            jax.ShapeDtypeStruct((SUBLANE, m), F32),
            jax.ShapeDtypeStruct((N_EXP, LANE), F32),
        ],
        scratch_shapes=[pltpu.VMEM((N_EXP, LANE), F32)],
        compiler_params=_params(1),
        name="norm_route",
    )(h, g.reshape(1, D), mod3, mod3, wr_pad, bias_col)


def _row_gather(idx_ref, src_hbm, buf, sem, slot, n_rows):
    def issue(j, carry):
        for queue in range(N_DMA_QUEUES):
            k = j * N_DMA_QUEUES + queue
            pltpu.make_async_copy(src_hbm.at[pl.ds(idx_ref[0, k], 1)], buf.at[slot, pl.ds(k, 1)],
                                  sem.at[slot]).start(priority=queue)
        return carry
    lax.fori_loop(0, n_rows // N_DMA_QUEUES, issue, 0)


def _row_gather_wait(src_hbm, buf, sem, slot, n_rows):
    pltpu.make_async_copy(src_hbm.at[pl.ds(0, n_rows)], buf.at[slot], sem.at[slot]).wait()


def _dispatch_kernel(nvalid_ref, idx_ref, idx_next_ref, f_hbm, o_ref, buf, sem):
    i = pl.program_id(0)
    tm = o_ref.shape[0]
    n_valid = nvalid_ref[0]
    slot = lax.rem(i, 2)

    @pl.when((i == 0) & (n_valid > 0))
    def _():
        _row_gather(idx_ref, f_hbm, buf, sem, 0, tm)

    @pl.when(i + 1 < n_valid)
    def _():
        _row_gather(idx_next_ref, f_hbm, buf, sem, 1 - slot, tm)

    @pl.when(i < n_valid)
    def _():
        _row_gather_wait(f_hbm, buf, sem, slot, tm)
        o_ref[...] = buf[slot].astype(o_ref.dtype)

    @pl.when(i >= n_valid)
    def _():
        o_ref[...] = jnp.zeros_like(o_ref)


def _dispatch(f, src_tiles, n_valid_tiles):
    nt = src_tiles.shape[0]
    idx_spec = pl.BlockSpec((None, 1, TM_EXP), lambda i, nv: (i, 0, 0), memory_space=pltpu.SMEM)
    idx_next_spec = pl.BlockSpec((None, 1, TM_EXP), lambda i, nv: (jnp.minimum(i + 1, nt - 1), 0, 0),
                                 memory_space=pltpu.SMEM)
    return pl.pallas_call(
        _dispatch_kernel,
        grid_spec=pltpu.PrefetchScalarGridSpec(
            num_scalar_prefetch=1,
            grid=(nt,),
            in_specs=[idx_spec, idx_next_spec, pl.BlockSpec(memory_space=pl.ANY)],
            out_specs=pl.BlockSpec((TM_EXP, D), lambda i, nv: (i, 0)),
            scratch_shapes=[pltpu.VMEM((2, TM_EXP, D), F32), pltpu.SemaphoreType.DMA((2,))],
        ),
        out_shape=jax.ShapeDtypeStruct((nt * TM_EXP, D), BF16),
        compiler_params=_params(1),
        name="moe_dispatch",
    )(n_valid_tiles, src_tiles, src_tiles, f)


(ST_EXPERT, ST_WCOL, ST_XROW, ST_OROW, ST_OCOL, ST_VALID, ST_FIRST, ST_SLOT, ST_NEXT_E, ST_NEXT_WCOL,
 ST_HAS_NEXT) = range(11)


def _gate_up_weights(w_hbm, wbuf, sem, layer, e, wcol, slot):
    nf = D_FF // TF_EXP
    col_g = pl.multiple_of(wcol * TF_EXP, TF_EXP)
    col_u = pl.multiple_of((nf + wcol) * TF_EXP, TF_EXP)
    return (pltpu.make_async_copy(w_hbm.at[layer, e, :, pl.ds(col_g, TF_EXP)], wbuf.at[slot, 0], sem.at[slot, 0]),
            pltpu.make_async_copy(w_hbm.at[layer, e, :, pl.ds(col_u, TF_EXP)], wbuf.at[slot, 1], sem.at[slot, 1]))


def _gate_up_kernel(st_ref, x_ref, w_hbm, o_ref, wbuf, sem, *, layer):
    s = pl.program_id(0)
    valid = st_ref[ST_VALID, s] > 0
    slot = st_ref[ST_SLOT, s]

    @pl.when(valid & (st_ref[ST_FIRST, s] > 0))
    def _():
        cur = _gate_up_weights(w_hbm, wbuf, sem, layer, st_ref[ST_EXPERT, s], st_ref[ST_WCOL, s], slot)

        @pl.when(s == 0)
        def _():
            for c in cur:
                c.start(priority=WEIGHT_QUEUE)

        for c in cur:
            c.wait()

        @pl.when(st_ref[ST_HAS_NEXT, s] > 0)
        def _():
            for c in _gate_up_weights(w_hbm, wbuf, sem, layer, st_ref[ST_NEXT_E, s], st_ref[ST_NEXT_WCOL, s],
                                      1 - slot):
                c.start(priority=WEIGHT_QUEUE)

    @pl.when(valid)
    def _():
        x = x_ref[...]
        gate = _dot(x, wbuf[slot, 0])
        up = _dot(x, wbuf[slot, 1])
        o_ref[...] = (gate * jax.nn.sigmoid(gate) * up).astype(o_ref.dtype)

    @pl.when(jnp.logical_not(valid))
    def _():
        o_ref[...] = jnp.zeros_like(o_ref)


def _gate_up(xs, w_gate_up, layer, steps):
    n_steps = steps.shape[1]
    return pl.pallas_call(
        functools.partial(_gate_up_kernel, layer=layer),
        grid_spec=pltpu.PrefetchScalarGridSpec(
            num_scalar_prefetch=1,
            grid=(n_steps,),
            in_specs=[
                pl.BlockSpec((TM_EXP, D), lambda s, st: (st[ST_XROW, s], 0)),
                pl.BlockSpec(memory_space=pl.ANY),
            ],
            out_specs=pl.BlockSpec((TM_EXP, TF_EXP), lambda s, st: (st[ST_OROW, s], st[ST_OCOL, s])),
            scratch_shapes=[pltpu.VMEM((2, 2, D, TF_EXP), F32), pltpu.SemaphoreType.DMA((2, 2))],
        ),
        out_shape=jax.ShapeDtypeStruct((xs.shape[0], D_FF), BF16),
        compiler_params=_params(1),
        name="moe_gate_up",
    )(steps, xs, w_gate_up)


TL_EXPERT, TL_SRC, TL_FIRST, TL_SLOT, TL_NEXT_E, TL_HAS_NEXT = range(6)


def _down_weights(w_hbm, wbuf, sem, layer, e, slot):
    return pltpu.make_async_copy(w_hbm.at[layer, e], wbuf.at[slot], sem.at[slot])


def _down_kernel(tl_ref, a_ref, w_hbm, o_ref, wbuf, sem, *, layer):
    i = pl.program_id(0)
    valid = tl_ref[TL_SRC, i] == i
    slot = tl_ref[TL_SLOT, i]

    @pl.when(valid & (tl_ref[TL_FIRST, i] > 0))
    def _():
        cur = _down_weights(w_hbm, wbuf, sem, layer, tl_ref[TL_EXPERT, i], slot)

        @pl.when(i == 0)
        def _():
            cur.start(priority=WEIGHT_QUEUE)

        cur.wait()

        @pl.when(tl_ref[TL_HAS_NEXT, i] > 0)
        def _():
            _down_weights(w_hbm, wbuf, sem, layer, tl_ref[TL_NEXT_E, i], 1 - slot).start(priority=WEIGHT_QUEUE)

    @pl.when(valid)
    def _():
        o_ref[...] = _dot(a_ref[...], wbuf[slot])

    @pl.when(jnp.logical_not(valid))
    def _():
        o_ref[...] = jnp.zeros_like(o_ref)


def _down(act, w_down, layer, tiles):
    nt = tiles.shape[1]
    return pl.pallas_call(
        functools.partial(_down_kernel, layer=layer),
        grid_spec=pltpu.PrefetchScalarGridSpec(
            num_scalar_prefetch=1,
            grid=(nt,),
            in_specs=[
                pl.BlockSpec((TM_EXP, D_FF), lambda i, tl: (tl[TL_SRC, i], 0)),
                pl.BlockSpec(memory_space=pl.ANY),
            ],
            out_specs=pl.BlockSpec((TM_EXP, D), lambda i, tl: (i, 0)),
            scratch_shapes=[pltpu.VMEM((2, D_FF, D), F32), pltpu.SemaphoreType.DMA((2,))],
        ),
        out_shape=jax.ShapeDtypeStruct((act.shape[0], D), F32),
        compiler_params=_params(1),
        name="moe_down",
    )(tiles, act, w_down)


def _combine_kernel(d1_ref, d2_ref, d1n_ref, d2n_ref, y_hbm, h_ref, w1_ref, w2_ref, gate_ref, *rest, tail):
    if tail == "next":
        ng_ref, nsh_ref, nsc_ref, o_ref, a_ref, buf1, buf2, sem1, sem2 = rest
    elif tail == "final":
        ng_ref, o_ref, buf1, buf2, sem1, sem2 = rest
    else:
        o_ref, buf1, buf2, sem1, sem2 = rest
    i = pl.program_id(0)
    n = pl.num_programs(0)
    tc = o_ref.shape[0]
    slot = lax.rem(i, 2)

    @pl.when(i == 0)
    def _():
        _row_gather(d1_ref, y_hbm, buf1, sem1, 0, tc)
        _row_gather(d2_ref, y_hbm, buf2, sem2, 0, tc)

    @pl.when(i + 1 < n)
    def _():
        _row_gather(d1n_ref, y_hbm, buf1, sem1, 1 - slot, tc)
        _row_gather(d2n_ref, y_hbm, buf2, sem2, 1 - slot, tc)

    _row_gather_wait(y_hbm, buf1, sem1, slot, tc)
    _row_gather_wait(y_hbm, buf2, sem2, slot, tc)
    moe = w1_ref[:, 0:1] * buf1[slot] + w2_ref[:, 0:1] * buf2[slot]
    out = h_ref[...] + gate_ref[...] * moe
    if tail == "final":
        out = _rms(out, ng_ref[...])
    o_ref[...] = out
    if tail == "next":
        a_ref[...] = (_rms(out, ng_ref[...]) * (1.0 + nsc_ref[...]) + nsh_ref[...]).astype(a_ref.dtype)


def _combine(y, d1_tiles, d2_tiles, h, w1b, w2b, mod3, layer, rows_per_cond, cond_base, tail, tail_g):
    m = h.shape[0]
    tc = TC_COMB
    nt = m // tc
    cur = pl.BlockSpec((None, 1, tc), lambda i: (i, 0, 0), memory_space=pltpu.SMEM)
    nxt = pl.BlockSpec((None, 1, tc), lambda i: (jnp.minimum(i + 1, nt - 1), 0, 0), memory_space=pltpu.SMEM)

    def mod_spec(lyr, which):
        return pl.BlockSpec((None, 1, D),
                            lambda i: (_mod_row(lyr, cond_base + (i * tc) // rows_per_cond, which), 0, 0))

    row_spec = pl.BlockSpec((tc, D), lambda i: (i, 0))
    in_specs = [cur, cur, nxt, nxt, pl.BlockSpec(memory_space=pl.ANY), row_spec,
                pl.BlockSpec((tc, LANE), lambda i: (i, 0)), pl.BlockSpec((tc, LANE), lambda i: (i, 0)),
                mod_spec(layer, 5)]
    args = [d1_tiles, d2_tiles, d1_tiles, d2_tiles, y, h, w1b, w2b, mod3]
    out_specs, out_shape = row_spec, jax.ShapeDtypeStruct((m, D), F32)
    if tail is not None:
        in_specs.append(pl.BlockSpec((1, D), lambda i: (0, 0)))
        args.append(tail_g.reshape(1, D))
    if tail == "next":
        in_specs += [mod_spec(layer + 1, 0), mod_spec(layer + 1, 1)]
        args += [mod3, mod3]
        out_specs = [row_spec, row_spec]
        out_shape = [out_shape, jax.ShapeDtypeStruct((m, D), BF16)]
    return pl.pallas_call(
        functools.partial(_combine_kernel, tail=tail),
        grid=(nt,),
        in_specs=in_specs,
        out_specs=out_specs,
        out_shape=out_shape,
        scratch_shapes=[pltpu.VMEM((2, tc, D), F32), pltpu.VMEM((2, tc, D), F32),
                        pltpu.SemaphoreType.DMA((2,)), pltpu.SemaphoreType.DMA((2,))],
        compiler_params=_params(1),
        name="moe_combine",
    )(*args)


def _moe_block(h, norm_g, mod3, layer, rows_per_cond, cond_base, wr_pad, bias_col, w_gate_up, w_down,
               tail=None, tail_g=None):
    t = h.shape[0]
    f, route, cnt = _route(h, norm_g, mod3, layer, rows_per_cond, cond_base, wr_pad, bias_col)

    i32 = jnp.int32
    e_ids = jnp.arange(N_EXP, dtype=i32)

    def take(table, idx):
        return jnp.sum(jnp.where(idx[:, None] == e_ids[None, :], table[None, :], 0), axis=1).astype(i32)

    def bucket(ends, pos):
        return jnp.sum((ends[None, :] <= pos[:, None]).astype(i32), axis=1)

    e1 = route[0].astype(i32)
    e2 = route[1].astype(i32)
    counts = cnt[:, 0].astype(i32)
    tiles_per_e = (counts + TM_EXP - 1) // TM_EXP
    tile_end = jnp.cumsum(tiles_per_e).astype(i32)
    tile_start = tile_end - tiles_per_e
    row_start = tile_start * TM_EXP
    d1 = take(row_start, e1) + route[4].astype(i32)
    d2 = take(row_start, e2) + route[5].astype(i32)
    nt = 2 * t // TM_EXP + N_EXP
    tok = jnp.arange(t, dtype=i32)
    src = jnp.zeros((nt * TM_EXP,), i32).at[jnp.concatenate([d1, d2])].set(jnp.concatenate([tok, tok]))
    n_valid = jnp.maximum(tile_end[-1], 1)

    nonempty = tiles_per_e > 0
    order = (jnp.cumsum(nonempty.astype(i32)) - nonempty.astype(i32)).astype(i32)
    later = nonempty[None, :] & (e_ids[None, :] > e_ids[:, None])
    next_e = jnp.min(jnp.where(later, e_ids[None, :], N_EXP), axis=1).astype(i32)

    tile_id = jnp.arange(nt, dtype=i32)
    tile_src = jnp.minimum(tile_id, n_valid - 1)
    tile_e = jnp.minimum(bucket(tile_end, tile_src), N_EXP - 1)
    tile_next = take(next_e, tile_e)
    tiles = jnp.stack([
        tile_e, tile_src,
        (tile_src == take(tile_start, tile_e)).astype(i32),
        take(order, tile_e) % 2,
        jnp.minimum(tile_next, N_EXP - 1),
        (tile_next < N_EXP).astype(i32),
    ]).astype(i32)

    nf = D_FF // TF_EXP
    step_id = jnp.arange(nf * nt, dtype=i32)
    step_valid = step_id < nf * n_valid
    sid = jnp.minimum(step_id, nf * n_valid - 1)
    step_e = jnp.minimum(bucket(nf * tile_end, sid), N_EXP - 1)
    local = sid - nf * take(tile_start, step_e)
    n_e = jnp.maximum(take(tiles_per_e, step_e), 1)
    step_f = local // n_e
    step_r = take(tile_start, step_e) + local % n_e
    pad = step_id - nf * n_valid
    last_col = step_f == nf - 1
    step_next_e = jnp.where(last_col, take(next_e, step_e), step_e)
    steps = jnp.stack([
        step_e, step_f, step_r,
        jnp.where(step_valid, step_r, n_valid + pad // nf),
        jnp.where(step_valid, step_f, pad % nf),
        step_valid.astype(i32),
        (local % n_e == 0).astype(i32),
        (nf * take(order, step_e) + step_f) % 2,
        jnp.minimum(step_next_e, N_EXP - 1),
        jnp.where(last_col, 0, step_f + 1),
        (step_next_e < N_EXP).astype(i32),
    ]).astype(i32)

    xs = _dispatch(f, src.reshape(nt, 1, TM_EXP), n_valid.reshape(1).astype(i32))
    act = _gate_up(xs, w_gate_up, layer, steps)
    y = _down(act, w_down, layer, tiles)

    ntc = t // TC_COMB
    w1b = jnp.broadcast_to(route[2][:, None], (t, LANE))
    w2b = jnp.broadcast_to(route[3][:, None], (t, LANE))
    return _combine(y, d1.reshape(ntc, 1, TC_COMB), d2.reshape(ntc, 1, TC_COMB), h, w1b, w2b,
                    mod3, layer, rows_per_cond, cond_base, tail, tail_g)


def _rope_tables(seq):
    rows = seq // GRID_W
    row = jnp.repeat(jnp.arange(rows), GRID_W).astype(F32)
    col = jnp.tile(jnp.arange(GRID_W), rows).astype(F32)
    n_freq = DH // 4
    inv_freq = ROPE_THETA ** (-jnp.arange(n_freq, dtype=F32) / n_freq)
    ang = jnp.concatenate([row[:, None] * inv_freq, col[:, None] * inv_freq], axis=-1)
    cos, sin = jnp.cos(ang), jnp.sin(ang)
    return jnp.concatenate([cos, cos], axis=-1), jnp.concatenate([-sin, sin], axis=-1)


def kernel(x, c, ctx, c_ctx, w_ada, b_ada, norm1_g, norm2_g, attn_w_qkv, attn_w_o, attn_sink, conv_w_in, conv_w,
           conv_w_out, w_router, router_bias, moe_w_gate_up, moe_w_down, final_g):
    batch, seq, _ = x.shape
    n_ctx = ctx.shape[1]
    depth = w_ada.shape[0]
    assert depth == 2 and x.shape[2] == D
    ctx_row = batch

    c8 = jnp.concatenate([c, c_ctx[None, :], jnp.zeros((SUBLANE - batch - 1, D), F32)], axis=0)
    mod3 = _ada(c8, w_ada, b_ada).reshape(depth * SUBLANE * N_MOD, 1, D)
    wr_f32 = jnp.pad(w_router, ((0, 0), (0, LANE - N_EXP)))
    wr_hi = wr_f32.astype(BF16)
    wr_pad = jnp.stack([wr_hi, (wr_f32 - wr_hi.astype(F32)).astype(BF16)])
    bias_col = router_bias.astype(F32).reshape(N_EXP, 1)
    rope = _rope_tables(seq)

    h_lat = x.reshape(batch * seq, D)
    h_ctx = ctx.reshape(batch * n_ctx, D)

    a_lat = _normmod(h_lat, norm1_g[0], mod3, 0, 0, seq, 0)
    a_ctx = _normmod(h_ctx, norm1_g[0], mod3, 0, 0, batch * n_ctx, ctx_row)
    qkv_lat = _qkv_proj(a_lat, attn_w_qkv[0], rope)
    qkv_ctx = _qkv_proj(a_ctx, attn_w_qkv[0])
    o_lat = _attention(qkv_lat, qkv_lat, qkv_ctx, attn_sink[0], batch, band=True)
    o_ctx = _attention(qkv_ctx, qkv_ctx, qkv_ctx, attn_sink[0], batch, band=False)
    h_lat = _proj_resid(o_lat, attn_w_o[0], h_lat, mod3, 0, 2, seq, 0)
    h_ctx = _proj_resid(o_ctx, attn_w_o[0], h_ctx, mod3, 0, 2, batch * n_ctx, ctx_row)
    moe_args = (wr_pad, bias_col, moe_w_gate_up, moe_w_down)
    h_lat, a_lat = _moe_block(h_lat, norm2_g[0], mod3, 0, seq, 0, *moe_args, "next", norm1_g[1])
    h_ctx, a_ctx = _moe_block(h_ctx, norm2_g[0], mod3, 0, batch * n_ctx, ctx_row, *moe_args, "next", norm1_g[1])

    b_gate, u = _conv_in(a_lat, conv_w_in[0])
    z = _conv_gate(b_gate, u, conv_w[0], seq)
    h_lat = _proj_resid(z, conv_w_out[0], h_lat, mod3, 1, 2, seq, 0)
    h_lat = _moe_block(h_lat, norm2_g[1], mod3, 1, seq, 0, *moe_args, "final", final_g)
    return h_lat.reshape(batch, seq, D)
```

```python
import functools

import jax
import jax.numpy as jnp
from jax import lax
from jax.experimental import pallas as pl
from jax.experimental.pallas import tpu as pltpu

D = 4096
N_HEADS = 32
N_KV = 8
GQA = 4
DH = 128
QKV = (N_HEADS + 2 * N_KV) * DH
GRID_W = 64
WINDOW = 128
ROPE_THETA = 10000.0
N_EXP = 16
N_GRP = 4
EXP_PER_GRP = 4
D_FF = 1024
N_MOD = 6
EPS = 1e-6
NEG = -1e30

LANE = 128
SUBLANE = 8
BF16_ROWS = 2 * SUBLANE
VMEM_LIMIT = 56 * 1024 * 1024
N_DMA_QUEUES = 2
WEIGHT_QUEUE = 1

TM_MM = 512
TN_MM = 1024
TN_CONV = 256
TM_CONV_IN = 1024
TM_ROW = 512
TM_CONV = 256
TM_EXP = 256
TF_EXP = 512
TC_COMB = 256

F32 = jnp.float32
BF16 = jnp.bfloat16


def _params(n_axes):
    return pltpu.CompilerParams(dimension_semantics=("arbitrary",) * n_axes,
                                vmem_limit_bytes=VMEM_LIMIT)


def _dot(a, b):
    return lax.dot_general(a, b, (((1,), (0,)), ((), ())), preferred_element_type=F32)


def _dot_nt(a, b):
    return lax.dot_general(a, b, (((1,), (1,)), ((), ())), preferred_element_type=F32)


def _ada_kernel(c_ref, w_ref, b_ref, o_ref):
    s = c_ref[...]
    s = (s * jax.nn.sigmoid(s)).astype(BF16)
    o_ref[...] = _dot(s, w_ref[...]) + b_ref[...]


def _ada(c8, w_ada, b_ada):
    depth = w_ada.shape[0]
    n = w_ada.shape[2]
    return pl.pallas_call(
        _ada_kernel,
        grid=(depth, n // TN_MM),
        in_specs=[
            pl.BlockSpec((SUBLANE, D), lambda l, j: (0, 0)),
            pl.BlockSpec((None, D, TN_MM), lambda l, j: (l, 0, j)),
            pl.BlockSpec((None, 1, TN_MM), lambda l, j: (l, 0, j)),
        ],
        out_specs=pl.BlockSpec((None, SUBLANE, TN_MM), lambda l, j: (l, 0, j)),
        out_shape=jax.ShapeDtypeStruct((depth, SUBLANE, n), F32),
        compiler_params=_params(2),
        name="ada_mod",
    )(c8, w_ada, b_ada.reshape(depth, 1, n))


def _mod_row(layer, cond_row, which):
    return (layer * SUBLANE + cond_row) * N_MOD + which


def _rms(x, g):
    return x * lax.rsqrt(jnp.mean(x * x, axis=-1, keepdims=True) + EPS) * g


def _normmod_kernel(h_ref, g_ref, sh_ref, sc_ref, o_ref):
    y = _rms(h_ref[...], g_ref[...])
    o_ref[...] = (y * (1.0 + sc_ref[...]) + sh_ref[...]).astype(o_ref.dtype)


def _normmod(h, g, mod3, layer, which_shift, rows_per_cond, cond_base):
    m = h.shape[0]
    tm = min(TM_ROW, rows_per_cond)

    def row(which):
        return lambda i: (_mod_row(layer, cond_base + (i * tm) // rows_per_cond, which), 0, 0)

    return pl.pallas_call(
        _normmod_kernel,
        grid=(m // tm,),
        in_specs=[
            pl.BlockSpec((tm, D), lambda i: (i, 0)),
            pl.BlockSpec((1, D), lambda i: (0, 0)),
            pl.BlockSpec((None, 1, D), row(which_shift)),
            pl.BlockSpec((None, 1, D), row(which_shift + 1)),
        ],
        out_specs=pl.BlockSpec((tm, D), lambda i: (i, 0)),
        out_shape=jax.ShapeDtypeStruct((m, D), BF16),
        compiler_params=_params(1),
        name="norm_modulate",
    )(h, g.reshape(1, D), mod3, mod3)


def _mm_plain_kernel(x_ref, w_ref, o_ref):
    o_ref[...] = _dot(x_ref[...], w_ref[...]).astype(o_ref.dtype)


def _mm_rope_kernel(x_ref, w_ref, cos_ref, sin_ref, o_ref, *, n_rope_tiles):
    acc = _dot(x_ref[...], w_ref[...])

    @pl.when(pl.program_id(0) < n_rope_tiles)
    def _():
        cosf = cos_ref[...]
        sinf = sin_ref[...]
        for hh in range(acc.shape[1] // DH):
            xh = acc[:, hh * DH:(hh + 1) * DH]
            o_ref[:, hh * DH:(hh + 1) * DH] = (
                xh * cosf + pltpu.roll(xh, DH // 2, axis=1) * sinf).astype(o_ref.dtype)

    @pl.when(pl.program_id(0) >= n_rope_tiles)
    def _():
        o_ref[...] = acc.astype(o_ref.dtype)


def _mm_resid_kernel(x_ref, w_ref, h_ref, gate_ref, o_ref):
    o_ref[...] = h_ref[...] + gate_ref[...] * _dot(x_ref[...], w_ref[...])


def _qkv_proj(a, w_qkv, rope=None):
    m = a.shape[0]
    grid = (QKV // TN_MM, m // TM_MM)
    x_spec = pl.BlockSpec((TM_MM, D), lambda j, i: (i, 0))
    w_spec = pl.BlockSpec((D, TN_MM), lambda j, i: (0, j))
    o_spec = pl.BlockSpec((TM_MM, TN_MM), lambda j, i: (i, j))
    out_shape = jax.ShapeDtypeStruct((m, QKV), BF16)
    if rope is None:
        return pl.pallas_call(_mm_plain_kernel, grid=grid, in_specs=[x_spec, w_spec], out_specs=o_spec,
                              out_shape=out_shape, compiler_params=_params(2), name="qkv_ctx")(a, w_qkv)
    cosf, sinf = rope
    seq_tiles = cosf.shape[0] // TM_MM
    t_spec = pl.BlockSpec((TM_MM, DH), lambda j, i: (i % seq_tiles, 0))
    n_rope_tiles = (N_HEADS + N_KV) * DH // TN_MM
    return pl.pallas_call(
        functools.partial(_mm_rope_kernel, n_rope_tiles=n_rope_tiles),
        grid=grid, in_specs=[x_spec, w_spec, t_spec, t_spec], out_specs=o_spec,
        out_shape=out_shape, compiler_params=_params(2), name="qkv_rope")(a, w_qkv, cosf, sinf)


def _proj_resid(x, w, h, mod3, layer, which_gate, rows_per_cond, cond_base):
    m = x.shape[0]
    tm = min(TM_MM, rows_per_cond)

    def gate_map(j, i):
        return (_mod_row(layer, cond_base + (i * tm) // rows_per_cond, which_gate), 0, j)

    return pl.pallas_call(
        _mm_resid_kernel,
        grid=(D // TN_MM, m // tm),
        in_specs=[
            pl.BlockSpec((tm, D), lambda j, i: (i, 0)),
            pl.BlockSpec((D, TN_MM), lambda j, i: (0, j)),
            pl.BlockSpec((tm, TN_MM), lambda j, i: (i, j)),
            pl.BlockSpec((None, 1, TN_MM), gate_map),
        ],
        out_specs=pl.BlockSpec((tm, TN_MM), lambda j, i: (i, j)),
        out_shape=jax.ShapeDtypeStruct((m, D), F32),
        compiler_params=_params(2),
        name="proj_residual",
    )(x, w, h, mod3)


def _attn_kernel(sink_ref, q_ref, k_ref, v_ref, kc_ref, vc_ref, o_ref, *scratch, band, seq):
    kh = pl.program_id(1)
    log2e = 1.4426950408889634
    c = DH ** -0.5 * log2e
    kc = kc_ref[...]
    vc = vc_ref[...]
    cols = GQA * WINDOW
    lane = lax.broadcasted_iota(jnp.int32, (1, cols), 1)
    sink_row = jnp.full((1, cols), sink_ref[kh * GQA + GQA - 1], F32)
    for g in range(GQA - 2, -1, -1):
        sink_row = jnp.where(lane < (g + 1) * WINDOW, sink_ref[kh * GQA + g], sink_row)
    sink_row = sink_row * log2e
    n_band = 3 * WINDOW if band else 0
    if band:
        (mask_ref,) = scratch
        key_minus_query = (lax.broadcasted_iota(jnp.int32, (n_band, cols), 0)
                           - (lax.broadcasted_iota(jnp.int32, (n_band, cols), 1) & (WINDOW - 1)))
        for t in range(3):
            valid = (key_minus_query >= (t - 1) * WINDOW) & (key_minus_query <= (t + 1) * WINDOW)
            mask_ref[t] = jnp.where(valid, 0.0, NEG)

    def block(n, carry):
        q0 = pl.multiple_of(n * WINDOW, WINDOW)
        qb = q_ref[pl.ds(q0, WINDOW), :]
        q4 = jnp.concatenate([qb[:, g * DH:(g + 1) * DH] for g in range(GQA)], axis=0)
        s_ctx = _dot_nt(kc, q4)
        m_raw = jnp.max(s_ctx, axis=0, keepdims=True)
        if band:
            ws = pl.multiple_of(jnp.clip((n - 1) * WINDOW, 0, seq - n_band), WINDOW)
            s_band = _dot_nt(k_ref[pl.ds(ws, n_band), :], q4) + mask_ref[(q0 - ws) // WINDOW]
            m_raw = jnp.maximum(m_raw, jnp.max(s_band, axis=0, keepdims=True))
        m = jnp.maximum(m_raw * c, sink_row)
        p_ctx = jnp.exp2(s_ctx * c - m)
        denom = jnp.sum(p_ctx, axis=0, keepdims=True) + jnp.exp2(sink_row - m)
        o_t = lax.dot_general(vc, p_ctx.astype(BF16), (((0,), (0,)), ((), ())), preferred_element_type=F32)
        if band:
            p_band = jnp.exp2(s_band * c - m)
            denom = denom + jnp.sum(p_band, axis=0, keepdims=True)
            o_t = o_t + lax.dot_general(v_ref[pl.ds(ws, n_band), :], p_band.astype(BF16),
                                        (((0,), (0,)), ((), ())), preferred_element_type=F32)
        o_t = o_t / denom
        for g in range(GQA):
            o_ref[pl.ds(q0, WINDOW), g * DH:(g + 1) * DH] = (
                o_t[:, g * WINDOW:(g + 1) * WINDOW].T.astype(o_ref.dtype))
        return carry

    lax.fori_loop(0, q_ref.shape[0] // WINDOW, block, 0, unroll=2)


def _attention(qkv_q, qkv_kv, qkv_ctx, sink, batch, band):
    seq_q = qkv_q.shape[0] // batch
    seq_k = qkv_kv.shape[0] // batch
    n_ctx = qkv_ctx.shape[0] // batch
    k_col = N_HEADS
    v_col = N_HEADS + N_KV
    return pl.pallas_call(
        functools.partial(_attn_kernel, band=band, seq=seq_k),
        grid_spec=pltpu.PrefetchScalarGridSpec(
            num_scalar_prefetch=1,
            grid=(batch, N_KV),
            in_specs=[
                pl.BlockSpec((seq_q, GQA * DH), lambda b, k, s: (b, k)),
                pl.BlockSpec((seq_k, DH), lambda b, k, s: (b, k_col + k)),
                pl.BlockSpec((seq_k, DH), lambda b, k, s: (b, v_col + k)),
                pl.BlockSpec((n_ctx, DH), lambda b, k, s: (b, k_col + k)),
                pl.BlockSpec((n_ctx, DH), lambda b, k, s: (b, v_col + k)),
            ],
            out_specs=pl.BlockSpec((seq_q, GQA * DH), lambda b, k, s: (b, k)),
            scratch_shapes=[pltpu.VMEM((3, 3 * WINDOW, GQA * WINDOW), F32)] if band else [],
        ),
        out_shape=jax.ShapeDtypeStruct((qkv_q.shape[0], D), BF16),
        compiler_params=_params(2),
        name="window_attn" if band else "ctx_attn",
    )(sink, qkv_q, qkv_kv, qkv_kv, qkv_ctx, qkv_ctx)


def _conv_in_kernel(x_ref, wb_ref, wc_ref, wx_ref, b_ref, u_ref):
    x = x_ref[...]
    b_ref[...] = _dot(x, wb_ref[...]).astype(b_ref.dtype)
    u_ref[...] = (_dot(x, wc_ref[...]) * _dot(x, wx_ref[...])).astype(u_ref.dtype)


def _conv_in(a, w_in):
    m = a.shape[0]
    nt = D // TN_CONV
    o_spec = pl.BlockSpec((TM_CONV_IN, TN_CONV), lambda j, i: (i, j))
    return pl.pallas_call(
        _conv_in_kernel,
        grid=(nt, m // TM_CONV_IN),
        in_specs=[
            pl.BlockSpec((TM_CONV_IN, D), lambda j, i: (i, 0)),
            pl.BlockSpec((D, TN_CONV), lambda j, i: (0, j)),
            pl.BlockSpec((D, TN_CONV), lambda j, i: (0, nt + j)),
            pl.BlockSpec((D, TN_CONV), lambda j, i: (0, 2 * nt + j)),
        ],
        out_specs=[o_spec, o_spec],
        out_shape=[jax.ShapeDtypeStruct((m, D), BF16)] * 2,
        compiler_params=_params(2),
        name="conv_in_proj",
    )(a, w_in, w_in, w_in)


def _conv_gate_kernel(b_ref, u_ref, up_ref, un_ref, w_ref, o_ref, *, seq):
    i = pl.program_id(0)
    tm = u_ref.shape[0]
    u = u_ref[...].astype(F32)
    t0 = i * tm
    prev_row = jnp.where(lax.rem(t0, seq) == 0, 0.0, up_ref[...].astype(F32)[BF16_ROWS - 1:BF16_ROWS, :])
    next_row = jnp.where(lax.rem(t0 + tm, seq) == 0, 0.0, un_ref[...].astype(F32)[0:1, :])
    row = lax.broadcasted_iota(jnp.int32, u.shape, 0)
    u_prev = jnp.where(row == 0, prev_row, pltpu.roll(u, 1, axis=0))
    u_next = jnp.where(row == tm - 1, next_row, pltpu.roll(u, tm - 1, axis=0))
    y = u_prev * w_ref[0:1, :] + u * w_ref[1:2, :] + u_next * w_ref[2:3, :]
    o_ref[...] = (b_ref[...].astype(F32) * y).astype(o_ref.dtype)


def _conv_gate(b, u, conv_w, seq):
    m = u.shape[0]
    tm = TM_CONV
    per = tm // BF16_ROWS
    last = m // BF16_ROWS - 1
    return pl.pallas_call(
        functools.partial(_conv_gate_kernel, seq=seq),
        grid=(m // tm,),
        in_specs=[
            pl.BlockSpec((tm, D), lambda i: (i, 0)),
            pl.BlockSpec((tm, D), lambda i: (i, 0)),
            pl.BlockSpec((BF16_ROWS, D), lambda i: (jnp.maximum(i * per - 1, 0), 0)),
            pl.BlockSpec((BF16_ROWS, D), lambda i: (jnp.minimum((i + 1) * per, last), 0)),
            pl.BlockSpec((3, D), lambda i: (0, 0)),
        ],
        out_specs=pl.BlockSpec((tm, D), lambda i: (i, 0)),
        out_shape=jax.ShapeDtypeStruct((m, D), BF16),
        compiler_params=_params(1),
        name="conv_gate",
    )(b, u, u, u, conv_w)


def _route_kernel(h_ref, g_ref, sh_ref, sc_ref, wr_ref, bias_ref, f_ref, route_ref, cnt_ref):
    tm = h_ref.shape[0]
    f = _rms(h_ref[...], g_ref[...]) * (1.0 + sc_ref[...]) + sh_ref[...]
    f_ref[...] = f.astype(f_ref.dtype)
    f_hi = f.astype(BF16)
    f_lo = (f - f_hi.astype(F32)).astype(BF16)
    logits = _dot(f_hi, wr_ref[0]) + (_dot(f_lo, wr_ref[0]) + _dot(f_hi, wr_ref[1]))
    lt = logits.T[:N_EXP]
    score = jax.nn.sigmoid(lt)
    sel = score + bias_ref[...]
    sel_rows = [sel[e:e + 1] for e in range(N_EXP)]
    score_rows = [score[e:e + 1] for e in range(N_EXP)]

    grp_scores = []
    for gi in range(N_GRP):
        a, b, c, d = sel_rows[gi * EXP_PER_GRP:(gi + 1) * EXP_PER_GRP]
        hi1, lo1 = jnp.maximum(a, b), jnp.minimum(a, b)
        hi2, lo2 = jnp.maximum(c, d), jnp.minimum(c, d)
        grp_scores.append(jnp.maximum(hi1, hi2) + jnp.maximum(jnp.minimum(hi1, hi2), jnp.maximum(lo1, lo2)))
    best = grp_scores[0]
    g_idx = jnp.zeros_like(best, dtype=jnp.int32)
    for gi in range(1, N_GRP):
        better = grp_scores[gi] > best
        g_idx = jnp.where(better, gi, g_idx)
        best = jnp.where(better, grp_scores[gi], best)

    masked = [jnp.where(g_idx == e // EXP_PER_GRP, sel_rows[e], NEG) for e in range(N_EXP)]
    v1 = masked[0]
    e1 = jnp.zeros_like(g_idx)
    for e in range(1, N_EXP):
        better = masked[e] > v1
        e1 = jnp.where(better, e, e1)
        v1 = jnp.where(better, masked[e], v1)
    v2 = jnp.full_like(v1, -jnp.inf)
    e2 = jnp.zeros_like(g_idx)
    for e in range(N_EXP):
        better = (masked[e] > v2) & (e1 != e)
        e2 = jnp.where(better, e, e2)
        v2 = jnp.where(better, masked[e], v2)
    s1 = jnp.zeros_like(v1)
    s2 = jnp.zeros_like(v1)
    for e in range(N_EXP):
        s1 = jnp.where(e1 == e, score_rows[e], s1)
        s2 = jnp.where(e2 == e, score_rows[e], s2)
    tot = s1 + s2
    w1 = s1 / tot
    w2 = s2 / tot

    e_iota = lax.broadcasted_iota(jnp.int32, (N_EXP, tm), 0)
    hit = (e_iota == e1) | (e_iota == e2)
    onehot = jnp.where(hit, 1.0, 0.0).astype(BF16)
    r_i = lax.broadcasted_iota(jnp.int32, (tm, tm), 0)
    c_i = lax.broadcasted_iota(jnp.int32, (tm, tm), 1)
    upper = jnp.where(r_i <= c_i, 1.0, 0.0).astype(BF16)
    rank = _dot(onehot, upper) - 1.0
    rank1 = jnp.sum(jnp.where(e_iota == e1, rank, 0.0), axis=0, keepdims=True)
    rank2 = jnp.sum(jnp.where(e_iota == e2, rank, 0.0), axis=0, keepdims=True)
    cnt_ref[...] = jnp.broadcast_to(jnp.sum(jnp.where(hit, 1.0, 0.0), axis=1, keepdims=True), cnt_ref.shape)

    route_ref[0:1, :] = e1.astype(F32)
    route_ref[1:2, :] = e2.astype(F32)
    route_ref[2:3, :] = w1
    route_ref[3:4, :] = w2
    route_ref[4:5, :] = rank1
    route_ref[5:6, :] = rank2
    route_ref[6:8, :] = jnp.zeros((2, tm), F32)


def _route(h, g, mod3, layer, rows_per_cond, cond_base, wr_pad, bias_col):
    m = h.shape[0]
    tm = min(TM_ROW, rows_per_cond)

    def row(which):
        return lambda i: (_mod_row(layer, cond_base + (i * tm) // rows_per_cond, which), 0, 0)

    return pl.pallas_call(
        _route_kernel,
        grid=(m // tm,),
        in_specs=[
            pl.BlockSpec((tm, D), lambda i: (i, 0)),
            pl.BlockSpec((1, D), lambda i: (0, 0)),
            pl.BlockSpec((None, 1, D), row(3)),
            pl.BlockSpec((None, 1, D), row(4)),
            pl.BlockSpec((2, D, LANE), lambda i: (0, 0, 0)),
            pl.BlockSpec((N_EXP, 1), lambda i: (0, 0)),
        ],
        out_specs=[
            pl.BlockSpec((tm, D), lambda i: (i, 0)),
            pl.BlockSpec((SUBLANE, tm), lambda i: (0, i)),
            pl.BlockSpec((None, N_EXP, LANE), lambda i: (i, 0, 0)),
        ],
        out_shape=[
            jax.ShapeDtypeStruct((m, D), BF16),
            jax.ShapeDtypeStruct((SUBLANE, m), F32),
            jax.ShapeDtypeStruct((m // tm, N_EXP, LANE), F32),
        ],
        compiler_params=_params(1),
        name="norm_route",
    )(h, g.reshape(1, D), mod3, mod3, wr_pad, bias_col)


def _row_gather(idx_ref, src_hbm, buf, sem, slot, n_rows):
    def issue(j, carry):
        for queue in range(N_DMA_QUEUES):
            k = j * N_DMA_QUEUES + queue
            pltpu.make_async_copy(src_hbm.at[pl.ds(idx_ref[0, k], 1)], buf.at[slot, pl.ds(k, 1)],
                                  sem.at[slot]).start(priority=queue)
        return carry
    lax.fori_loop(0, n_rows // N_DMA_QUEUES, issue, 0)


def _row_gather_wait(src_hbm, buf, sem, slot, n_rows):
    pltpu.make_async_copy(src_hbm.at[pl.ds(0, n_rows)], buf.at[slot], sem.at[slot]).wait()


SEG_ALIGN = BF16_ROWS
SEG_SIZES = (512, 256, 128, 64, 32, 16)
TAIL_SIZES = (128, 64, 32, 16)
SORT_ROWS = 2 * TM_ROW + N_EXP * SEG_ALIGN


def _seg_copies(src, src_off, dst_hbm, dst_off, length, sem, sizes):
    out = []
    for size in sizes:
        take = (length & size) != 0
        cp = pltpu.make_async_copy(src.at[pl.ds(pl.multiple_of(src_off, SEG_ALIGN), size)],
                                   dst_hbm.at[pl.ds(pl.multiple_of(dst_off, SEG_ALIGN), size)], sem)
        out.append((take, cp))
        src_off = src_off + jnp.where(take, size, 0)
        dst_off = dst_off + jnp.where(take, size, 0)
    return out


def _run_copies(copies):
    for take, cp in copies:
        @pl.when(take)
        def _():
            cp.start()
    for take, cp in copies:
        @pl.when(take)
        def _():
            cp.wait()


def _sort_dispatch_kernel(seg_ref, tail_ref, f_ref, pos_ref, xs_hbm, sbuf, zbuf, sem, *, n_slot_tiles):
    i = pl.program_id(0)
    tm = f_ref.shape[0]
    slot = lax.broadcasted_iota(jnp.int32, (SORT_ROWS, tm), 0)
    onehot = jnp.where((slot == pos_ref[0:1, :]) | (slot == pos_ref[1:2, :]), 1.0, 0.0).astype(BF16)
    for cb in range(D // TN_MM):
        cols = slice(cb * TN_MM, (cb + 1) * TN_MM)
        sbuf[:, cols] = _dot(onehot, f_ref[:, cols]).astype(BF16)
    copies = []
    for e in range(N_EXP):
        copies += _seg_copies(sbuf, seg_ref[i, e], xs_hbm, seg_ref[i, 2 * N_EXP + e], seg_ref[i, N_EXP + e],
                              sem, SEG_SIZES)
    _run_copies(copies)

    @pl.when(i == pl.num_programs(0) - 1)
    def _():
        zbuf[...] = jnp.zeros_like(zbuf)
        tails = []
        for e in range(N_EXP):
            tails += _seg_copies(zbuf, 0, xs_hbm, tail_ref[e], tail_ref[N_EXP + e], sem, TAIL_SIZES)
        _run_copies(tails)

        def unused_tile(t):
            return pltpu.make_async_copy(zbuf, xs_hbm.at[pl.ds(pl.multiple_of(t * TM_EXP, TM_EXP), TM_EXP)], sem)

        def start(t, carry):
            unused_tile(t).start()
            return carry

        def wait(t, carry):
            unused_tile(t).wait()
            return carry

        lax.fori_loop(tail_ref[2 * N_EXP], n_slot_tiles, start, 0)
        lax.fori_loop(tail_ref[2 * N_EXP], n_slot_tiles, wait, 0)


def _sort_dispatch(f, pos, seg, tail, n_slot_tiles):
    t = f.shape[0]
    return pl.pallas_call(
        functools.partial(_sort_dispatch_kernel, n_slot_tiles=n_slot_tiles),
        grid_spec=pltpu.PrefetchScalarGridSpec(
            num_scalar_prefetch=2,
            grid=(t // TM_ROW,),
            in_specs=[
                pl.BlockSpec((TM_ROW, D), lambda i, sg, tl: (i, 0)),
                pl.BlockSpec((SUBLANE, TM_ROW), lambda i, sg, tl: (0, i)),
            ],
            out_specs=pl.BlockSpec(memory_space=pl.ANY),
            scratch_shapes=[pltpu.VMEM((SORT_ROWS, D), BF16), pltpu.VMEM((TM_EXP, D), BF16),
                            pltpu.SemaphoreType.DMA(())],
        ),
        out_shape=jax.ShapeDtypeStruct((n_slot_tiles * TM_EXP, D), BF16),
        compiler_params=_params(1),
        name="moe_sort_dispatch",
    )(seg, tail, f, pos)


(ST_EXPERT, ST_WCOL, ST_XROW, ST_OROW, ST_OCOL, ST_VALID, ST_FIRST, ST_SLOT, ST_NEXT_E, ST_NEXT_WCOL,
 ST_HAS_NEXT) = range(11)


def _gate_up_weights(w_hbm, wbuf, sem, layer, e, wcol, slot):
    nf = D_FF // TF_EXP
    col_g = pl.multiple_of(wcol * TF_EXP, TF_EXP)
    col_u = pl.multiple_of((nf + wcol) * TF_EXP, TF_EXP)
    return (pltpu.make_async_copy(w_hbm.at[layer, e, :, pl.ds(col_g, TF_EXP)], wbuf.at[slot, 0], sem.at[slot, 0]),
            pltpu.make_async_copy(w_hbm.at[layer, e, :, pl.ds(col_u, TF_EXP)], wbuf.at[slot, 1], sem.at[slot, 1]))


def _gate_up_kernel(st_ref, x_ref, w_hbm, o_ref, wbuf, sem, *, layer):
    s = pl.program_id(0)
    valid = st_ref[ST_VALID, s] > 0
    slot = st_ref[ST_SLOT, s]

    @pl.when(valid & (st_ref[ST_FIRST, s] > 0))
    def _():
        cur = _gate_up_weights(w_hbm, wbuf, sem, layer, st_ref[ST_EXPERT, s], st_ref[ST_WCOL, s], slot)

        @pl.when(s == 0)
        def _():
            for c in cur:
                c.start(priority=WEIGHT_QUEUE)

        for c in cur:
            c.wait()

        @pl.when(st_ref[ST_HAS_NEXT, s] > 0)
        def _():
            for c in _gate_up_weights(w_hbm, wbuf, sem, layer, st_ref[ST_NEXT_E, s], st_ref[ST_NEXT_WCOL, s],
                                      1 - slot):
                c.start(priority=WEIGHT_QUEUE)

    @pl.when(valid)
    def _():
        x = x_ref[...]
        gate = _dot(x, wbuf[slot, 0])
        up = _dot(x, wbuf[slot, 1])
        o_ref[...] = (gate * jax.nn.sigmoid(gate) * up).astype(o_ref.dtype)

    @pl.when(jnp.logical_not(valid))
    def _():
        o_ref[...] = jnp.zeros_like(o_ref)


def _gate_up(xs, w_gate_up, layer, steps):
    n_steps = steps.shape[1]
    return pl.pallas_call(
        functools.partial(_gate_up_kernel, layer=layer),
        grid_spec=pltpu.PrefetchScalarGridSpec(
            num_scalar_prefetch=1,
            grid=(n_steps,),
            in_specs=[
                pl.BlockSpec((TM_EXP, D), lambda s, st: (st[ST_XROW, s], 0)),
                pl.BlockSpec(memory_space=pl.ANY),
            ],
            out_specs=pl.BlockSpec((TM_EXP, TF_EXP), lambda s, st: (st[ST_OROW, s], st[ST_OCOL, s])),
            scratch_shapes=[pltpu.VMEM((2, 2, D, TF_EXP), F32), pltpu.SemaphoreType.DMA((2, 2))],
        ),
        out_shape=jax.ShapeDtypeStruct((xs.shape[0], D_FF), BF16),
        compiler_params=_params(1),
        name="moe_gate_up",
    )(steps, xs, w_gate_up)


TL_EXPERT, TL_SRC, TL_FIRST, TL_SLOT, TL_NEXT_E, TL_HAS_NEXT = range(6)


def _down_weights(w_hbm, wbuf, sem, layer, e, slot):
    return pltpu.make_async_copy(w_hbm.at[layer, e], wbuf.at[slot], sem.at[slot])


def _down_kernel(tl_ref, a_ref, w_hbm, o_ref, wbuf, sem, *, layer):
    i = pl.program_id(0)
    valid = tl_ref[TL_SRC, i] == i
    slot = tl_ref[TL_SLOT, i]

    @pl.when(valid & (tl_ref[TL_FIRST, i] > 0))
    def _():
        cur = _down_weights(w_hbm, wbuf, sem, layer, tl_ref[TL_EXPERT, i], slot)

        @pl.when(i == 0)
        def _():
            cur.start(priority=WEIGHT_QUEUE)

        cur.wait()

        @pl.when(tl_ref[TL_HAS_NEXT, i] > 0)
        def _():
            _down_weights(w_hbm, wbuf, sem, layer, tl_ref[TL_NEXT_E, i], 1 - slot).start(priority=WEIGHT_QUEUE)

    @pl.when(valid)
    def _():
        o_ref[...] = _dot(a_ref[...], wbuf[slot])

    @pl.when(jnp.logical_not(valid))
    def _():
        o_ref[...] = jnp.zeros_like(o_ref)


def _down(act, w_down, layer, tiles):
    nt = tiles.shape[1]
    return pl.pallas_call(
        functools.partial(_down_kernel, layer=layer),
        grid_spec=pltpu.PrefetchScalarGridSpec(
            num_scalar_prefetch=1,
            grid=(nt,),
            in_specs=[
                pl.BlockSpec((TM_EXP, D_FF), lambda i, tl: (tl[TL_SRC, i], 0)),
                pl.BlockSpec(memory_space=pl.ANY),
            ],
            out_specs=pl.BlockSpec((TM_EXP, D), lambda i, tl: (i, 0)),
            scratch_shapes=[pltpu.VMEM((2, D_FF, D), F32), pltpu.SemaphoreType.DMA((2,))],
        ),
        out_shape=jax.ShapeDtypeStruct((act.shape[0], D), F32),
        compiler_params=_params(1),
        name="moe_down",
    )(tiles, act, w_down)


def _combine_kernel(d1_ref, d2_ref, d1n_ref, d2n_ref, y_hbm, h_ref, w1_ref, w2_ref, gate_ref, *rest, tail):
    if tail == "next":
        ng_ref, nsh_ref, nsc_ref, o_ref, a_ref, buf1, buf2, sem1, sem2 = rest
    elif tail == "final":
        ng_ref, o_ref, buf1, buf2, sem1, sem2 = rest
    else:
        o_ref, buf1, buf2, sem1, sem2 = rest
    i = pl.program_id(0)
    n = pl.num_programs(0)
    tc = o_ref.shape[0]
    slot = lax.rem(i, 2)

    @pl.when(i == 0)
    def _():
        _row_gather(d1_ref, y_hbm, buf1, sem1, 0, tc)
        _row_gather(d2_ref, y_hbm, buf2, sem2, 0, tc)

    @pl.when(i + 1 < n)
    def _():
        _row_gather(d1n_ref, y_hbm, buf1, sem1, 1 - slot, tc)
        _row_gather(d2n_ref, y_hbm, buf2, sem2, 1 - slot, tc)

    _row_gather_wait(y_hbm, buf1, sem1, slot, tc)
    _row_gather_wait(y_hbm, buf2, sem2, slot, tc)
    moe = w1_ref[:, 0:1] * buf1[slot] + w2_ref[:, 0:1] * buf2[slot]
    out = h_ref[...] + gate_ref[...] * moe
    if tail == "final":
        out = _rms(out, ng_ref[...])
    o_ref[...] = out
    if tail == "next":
        a_ref[...] = (_rms(out, ng_ref[...]) * (1.0 + nsc_ref[...]) + nsh_ref[...]).astype(a_ref.dtype)


def _combine(y, d1_tiles, d2_tiles, h, w1b, w2b, mod3, layer, rows_per_cond, cond_base, tail, tail_g):
    m = h.shape[0]
    tc = TC_COMB
    nt = m // tc
    cur = pl.BlockSpec((None, 1, tc), lambda i: (i, 0, 0), memory_space=pltpu.SMEM)
    nxt = pl.BlockSpec((None, 1, tc), lambda i: (jnp.minimum(i + 1, nt - 1), 0, 0), memory_space=pltpu.SMEM)

    def mod_spec(lyr, which):
        return pl.BlockSpec((None, 1, D),
                            lambda i: (_mod_row(lyr, cond_base + (i * tc) // rows_per_cond, which), 0, 0))

    row_spec = pl.BlockSpec((tc, D), lambda i: (i, 0))
    in_specs = [cur, cur, nxt, nxt, pl.BlockSpec(memory_space=pl.ANY), row_spec,
                pl.BlockSpec((tc, LANE), lambda i: (i, 0)), pl.BlockSpec((tc, LANE), lambda i: (i, 0)),
                mod_spec(layer, 5)]
    args = [d1_tiles, d2_tiles, d1_tiles, d2_tiles, y, h, w1b, w2b, mod3]
    out_specs, out_shape = row_spec, jax.ShapeDtypeStruct((m, D), F32)
    if tail is not None:
        in_specs.append(pl.BlockSpec((1, D), lambda i: (0, 0)))
        args.append(tail_g.reshape(1, D))
    if tail == "next":
        in_specs += [mod_spec(layer + 1, 0), mod_spec(layer + 1, 1)]
        args += [mod3, mod3]
        out_specs = [row_spec, row_spec]
        out_shape = [out_shape, jax.ShapeDtypeStruct((m, D), BF16)]
    return pl.pallas_call(
        functools.partial(_combine_kernel, tail=tail),
        grid=(nt,),
        in_specs=in_specs,
        out_specs=out_specs,
        out_shape=out_shape,
        scratch_shapes=[pltpu.VMEM((2, tc, D), F32), pltpu.VMEM((2, tc, D), F32),
                        pltpu.SemaphoreType.DMA((2,)), pltpu.SemaphoreType.DMA((2,))],
        compiler_params=_params(1),
        name="moe_combine",
    )(*args)


def _moe_block(h, norm_g, mod3, layer, rows_per_cond, cond_base, wr_pad, bias_col, w_gate_up, w_down,
               tail=None, tail_g=None):
    t = h.shape[0]
    f, route, cnt = _route(h, norm_g, mod3, layer, rows_per_cond, cond_base, wr_pad, bias_col)

    i32 = jnp.int32
    e_ids = jnp.arange(N_EXP, dtype=i32)

    def take(table, idx):
        return jnp.sum(jnp.where(idx[:, None] == e_ids[None, :], table[None, :], 0), axis=1).astype(i32)

    def bucket(ends, pos):
        return jnp.sum((ends[None, :] <= pos[:, None]).astype(i32), axis=1)

    n_tok_tiles = t // TM_ROW
    cnt_tile = cnt[:, :, 0].astype(i32)
    seg_len = (cnt_tile + SEG_ALIGN - 1) // SEG_ALIGN * SEG_ALIGN
    rows_e = jnp.sum(seg_len, axis=0)
    tiles_per_e = (rows_e + TM_EXP - 1) // TM_EXP
    tile_end = jnp.cumsum(tiles_per_e).astype(i32)
    tile_start = tile_end - tiles_per_e
    row_start = tile_start * TM_EXP
    seg_out = row_start[None, :] + jnp.cumsum(seg_len, axis=0) - seg_len
    seg_in = jnp.cumsum(seg_len, axis=1) - seg_len
    seg = jnp.concatenate([seg_in, seg_len, seg_out], axis=1).astype(i32)

    def take_tile(table, e):
        e_tiles = e.reshape(n_tok_tiles, TM_ROW)
        hit = e_tiles[:, :, None] == e_ids[None, None, :]
        return jnp.sum(jnp.where(hit, table[:, None, :], 0), axis=2).reshape(t).astype(i32)

    e1 = route[0].astype(i32)
    e2 = route[1].astype(i32)
    r1 = route[4].astype(i32)
    r2 = route[5].astype(i32)
    pos = jnp.zeros((SUBLANE, t), i32).at[0].set(take_tile(seg_in, e1) + r1).at[1].set(take_tile(seg_in, e2) + r2)
    d1 = take_tile(seg_out, e1) + r1
    d2 = take_tile(seg_out, e2) + r2
    nt = (2 * t + n_tok_tiles * N_EXP * SEG_ALIGN) // TM_EXP + N_EXP
    n_valid = jnp.maximum(tile_end[-1], 1)
    tail_tab = jnp.concatenate([row_start + rows_e, tiles_per_e * TM_EXP - rows_e, tile_end[-1:]]).astype(i32)

    nonempty = tiles_per_e > 0
    order = (jnp.cumsum(nonempty.astype(i32)) - nonempty.astype(i32)).astype(i32)
    later = nonempty[None, :] & (e_ids[None, :] > e_ids[:, None])
    next_e = jnp.min(jnp.where(later, e_ids[None, :], N_EXP), axis=1).astype(i32)

    tile_id = jnp.arange(nt, dtype=i32)
    tile_src = jnp.minimum(tile_id, n_valid - 1)
    tile_e = jnp.minimum(bucket(tile_end, tile_src), N_EXP - 1)
    tile_next = take(next_e, tile_e)
    tiles = jnp.stack([
        tile_e, tile_src,
        (tile_src == take(tile_start, tile_e)).astype(i32),
        take(order, tile_e) % 2,
        jnp.minimum(tile_next, N_EXP - 1),
        (tile_next < N_EXP).astype(i32),
    ]).astype(i32)

    nf = D_FF // TF_EXP
    step_id = jnp.arange(nf * nt, dtype=i32)
    step_valid = step_id < nf * n_valid
    sid = jnp.minimum(step_id, nf * n_valid - 1)
    step_e = jnp.minimum(bucket(nf * tile_end, sid), N_EXP - 1)
    local = sid - nf * take(tile_start, step_e)
    n_e = jnp.maximum(take(tiles_per_e, step_e), 1)
    step_f = local // n_e
    step_r = take(tile_start, step_e) + local % n_e
    pad = step_id - nf * n_valid
    last_col = step_f == nf - 1
    step_next_e = jnp.where(last_col, take(next_e, step_e), step_e)
    steps = jnp.stack([
        step_e, step_f, step_r,
        jnp.where(step_valid, step_r, n_valid + pad // nf),
        jnp.where(step_valid, step_f, pad % nf),
        step_valid.astype(i32),
        (local % n_e == 0).astype(i32),
        (nf * take(order, step_e) + step_f) % 2,
        jnp.minimum(step_next_e, N_EXP - 1),
        jnp.where(last_col, 0, step_f + 1),
        (step_next_e < N_EXP).astype(i32),
    ]).astype(i32)

    xs = _sort_dispatch(f, pos, seg, tail_tab, nt)
    act = _gate_up(xs, w_gate_up, layer, steps)
    y = _down(act, w_down, layer, tiles)

    ntc = t // TC_COMB
    w1b = jnp.broadcast_to(route[2][:, None], (t, LANE))
    w2b = jnp.broadcast_to(route[3][:, None], (t, LANE))
    return _combine(y, d1.reshape(ntc, 1, TC_COMB), d2.reshape(ntc, 1, TC_COMB), h, w1b, w2b,
                    mod3, layer, rows_per_cond, cond_base, tail, tail_g)


def _rope_tables(seq):
    rows = seq // GRID_W
    row = jnp.repeat(jnp.arange(rows), GRID_W).astype(F32)
    col = jnp.tile(jnp.arange(GRID_W), rows).astype(F32)
    n_freq = DH // 4
    inv_freq = ROPE_THETA ** (-jnp.arange(n_freq, dtype=F32) / n_freq)
    ang = jnp.concatenate([row[:, None] * inv_freq, col[:, None] * inv_freq], axis=-1)
    cos, sin = jnp.cos(ang), jnp.sin(ang)
    return jnp.concatenate([cos, cos], axis=-1), jnp.concatenate([-sin, sin], axis=-1)


def kernel(x, c, ctx, c_ctx, w_ada, b_ada, norm1_g, norm2_g, attn_w_qkv, attn_w_o, attn_sink, conv_w_in, conv_w,
           conv_w_out, w_router, router_bias, moe_w_gate_up, moe_w_down, final_g):
    batch, seq, _ = x.shape
    n_ctx = ctx.shape[1]
    depth = w_ada.shape[0]
    assert depth == 2 and x.shape[2] == D
    ctx_row = batch

    c8 = jnp.concatenate([c, c_ctx[None, :], jnp.zeros((SUBLANE - batch - 1, D), F32)], axis=0)
    mod3 = _ada(c8, w_ada, b_ada).reshape(depth * SUBLANE * N_MOD, 1, D)
    wr_f32 = jnp.pad(w_router, ((0, 0), (0, LANE - N_EXP)))
    wr_hi = wr_f32.astype(BF16)
    wr_pad = jnp.stack([wr_hi, (wr_f32 - wr_hi.astype(F32)).astype(BF16)])
    bias_col = router_bias.astype(F32).reshape(N_EXP, 1)
    rope = _rope_tables(seq)

    h_lat = x.reshape(batch * seq, D)
    h_ctx = ctx.reshape(batch * n_ctx, D)

    a_lat = _normmod(h_lat, norm1_g[0], mod3, 0, 0, seq, 0)
    a_ctx = _normmod(h_ctx, norm1_g[0], mod3, 0, 0, batch * n_ctx, ctx_row)
    qkv_lat = _qkv_proj(a_lat, attn_w_qkv[0], rope)
    qkv_ctx = _qkv_proj(a_ctx, attn_w_qkv[0])
    o_lat = _attention(qkv_lat, qkv_lat, qkv_ctx, attn_sink[0], batch, band=True)
    o_ctx = _attention(qkv_ctx, qkv_ctx, qkv_ctx, attn_sink[0], batch, band=False)
    h_lat = _proj_resid(o_lat, attn_w_o[0], h_lat, mod3, 0, 2, seq, 0)
    h_ctx = _proj_resid(o_ctx, attn_w_o[0], h_ctx, mod3, 0, 2, batch * n_ctx, ctx_row)
    moe_args = (wr_pad, bias_col, moe_w_gate_up, moe_w_down)
    h_lat, a_lat = _moe_block(h_lat, norm2_g[0], mod3, 0, seq, 0, *moe_args, "next", norm1_g[1])
    h_ctx, a_ctx = _moe_block(h_ctx, norm2_g[0], mod3, 0, batch * n_ctx, ctx_row, *moe_args, "next", norm1_g[1])

    b_gate, u = _conv_in(a_lat, conv_w_in[0])
    z = _conv_gate(b_gate, u, conv_w[0], seq)
    h_lat = _proj_resid(z, conv_w_out[0], h_lat, mod3, 1, 2, seq, 0)
    h_lat = _moe_block(h_lat, norm2_g[1], mod3, 1, seq, 0, *moe_args, "final", final_g)
    return h_lat.reshape(batch, seq, D)
```

```python
import functools

import jax
import jax.numpy as jnp
from jax import lax
from jax.experimental import pallas as pl
from jax.experimental.pallas import tpu as pltpu

D = 4096
N_HEADS = 32
N_KV = 8
GQA = 4
DH = 128
QKV = (N_HEADS + 2 * N_KV) * DH
GRID_W = 64
WINDOW = 128
ROPE_THETA = 10000.0
N_EXP = 16
N_GRP = 4
EXP_PER_GRP = 4
D_FF = 1024
N_MOD = 6
EPS = 1e-6
NEG = -1e30

LANE = 128
SUBLANE = 8
BF16_ROWS = 2 * SUBLANE
VMEM_LIMIT = 56 * 1024 * 1024
WEIGHT_QUEUE = 1

TM_MM = 512
TN_MM = 1024
TN_CONV = 256
TM_CONV_IN = 1024
TM_ROW = 512
TM_CONV = 256
TM_EXP = 256
TF_EXP = 512

F32 = jnp.float32
BF16 = jnp.bfloat16


def _params(n_axes):
    return pltpu.CompilerParams(dimension_semantics=("arbitrary",) * n_axes,
                                vmem_limit_bytes=VMEM_LIMIT)


def _dot(a, b):
    return lax.dot_general(a, b, (((1,), (0,)), ((), ())), preferred_element_type=F32)


def _dot_nt(a, b):
    return lax.dot_general(a, b, (((1,), (1,)), ((), ())), preferred_element_type=F32)


def _ada_kernel(c_ref, w_ref, b_ref, o_ref):
    s = c_ref[...]
    s = (s * jax.nn.sigmoid(s)).astype(BF16)
    o_ref[...] = _dot(s, w_ref[...]) + b_ref[...]


def _ada(c8, w_ada, b_ada):
    depth = w_ada.shape[0]
    n = w_ada.shape[2]
    return pl.pallas_call(
        _ada_kernel,
        grid=(depth, n // TN_MM),
        in_specs=[
            pl.BlockSpec((SUBLANE, D), lambda l, j: (0, 0)),
            pl.BlockSpec((None, D, TN_MM), lambda l, j: (l, 0, j)),
            pl.BlockSpec((None, 1, TN_MM), lambda l, j: (l, 0, j)),
        ],
        out_specs=pl.BlockSpec((None, SUBLANE, TN_MM), lambda l, j: (l, 0, j)),
        out_shape=jax.ShapeDtypeStruct((depth, SUBLANE, n), F32),
        compiler_params=_params(2),
        name="ada_mod",
    )(c8, w_ada, b_ada.reshape(depth, 1, n))


def _mod_row(layer, cond_row, which):
    return (layer * SUBLANE + cond_row) * N_MOD + which


def _rms(x, g):
    return x * lax.rsqrt(jnp.mean(x * x, axis=-1, keepdims=True) + EPS) * g


def _normmod_kernel(h_ref, g_ref, sh_ref, sc_ref, o_ref):
    y = _rms(h_ref[...], g_ref[...])
    o_ref[...] = (y * (1.0 + sc_ref[...]) + sh_ref[...]).astype(o_ref.dtype)


def _normmod(h, g, mod3, layer, which_shift, rows_per_cond, cond_base):
    m = h.shape[0]
    tm = min(TM_ROW, rows_per_cond)

    def row(which):
        return lambda i: (_mod_row(layer, cond_base + (i * tm) // rows_per_cond, which), 0, 0)

    return pl.pallas_call(
        _normmod_kernel,
        grid=(m // tm,),
        in_specs=[
            pl.BlockSpec((tm, D), lambda i: (i, 0)),
            pl.BlockSpec((1, D), lambda i: (0, 0)),
            pl.BlockSpec((None, 1, D), row(which_shift)),
            pl.BlockSpec((None, 1, D), row(which_shift + 1)),
        ],
        out_specs=pl.BlockSpec((tm, D), lambda i: (i, 0)),
        out_shape=jax.ShapeDtypeStruct((m, D), BF16),
        compiler_params=_params(1),
        name="norm_modulate",
    )(h, g.reshape(1, D), mod3, mod3)


def _mm_plain_kernel(x_ref, w_ref, o_ref):
    o_ref[...] = _dot(x_ref[...], w_ref[...]).astype(o_ref.dtype)


def _mm_rope_kernel(x_ref, w_ref, cos_ref, sin_ref, o_ref, *, n_rope_tiles):
    acc = _dot(x_ref[...], w_ref[...])

    @pl.when(pl.program_id(0) < n_rope_tiles)
    def _():
        cosf = cos_ref[...]
        sinf = sin_ref[...]
        for hh in range(acc.shape[1] // DH):
            xh = acc[:, hh * DH:(hh + 1) * DH]
            o_ref[:, hh * DH:(hh + 1) * DH] = (
                xh * cosf + pltpu.roll(xh, DH // 2, axis=1) * sinf).astype(o_ref.dtype)

    @pl.when(pl.program_id(0) >= n_rope_tiles)
    def _():
        o_ref[...] = acc.astype(o_ref.dtype)


def _mm_resid_kernel(x_ref, w_ref, h_ref, gate_ref, o_ref):
    o_ref[...] = h_ref[...] + gate_ref[...] * _dot(x_ref[...], w_ref[...])


def _qkv_proj(a, w_qkv, rope=None, col_tiles=(0, QKV // TN_MM)):
    m = a.shape[0]
    first, count = col_tiles
    grid = (count, m // TM_MM)
    x_spec = pl.BlockSpec((TM_MM, D), lambda j, i: (i, 0))
    w_spec = pl.BlockSpec((D, TN_MM), lambda j, i: (0, first + j))
    o_spec = pl.BlockSpec((TM_MM, TN_MM), lambda j, i: (i, j))
    out_shape = jax.ShapeDtypeStruct((m, count * TN_MM), BF16)
    if rope is None:
        return pl.pallas_call(_mm_plain_kernel, grid=grid, in_specs=[x_spec, w_spec], out_specs=o_spec,
                              out_shape=out_shape, compiler_params=_params(2), name="qkv_ctx")(a, w_qkv)
    assert col_tiles == (0, QKV // TN_MM)
    cosf, sinf = rope
    seq_tiles = cosf.shape[0] // TM_MM
    t_spec = pl.BlockSpec((TM_MM, DH), lambda j, i: (i % seq_tiles, 0))
    n_rope_tiles = (N_HEADS + N_KV) * DH // TN_MM
    return pl.pallas_call(
        functools.partial(_mm_rope_kernel, n_rope_tiles=n_rope_tiles),
        grid=grid, in_specs=[x_spec, w_spec, t_spec, t_spec], out_specs=o_spec,
        out_shape=out_shape, compiler_params=_params(2), name="qkv_rope")(a, w_qkv, cosf, sinf)


def _proj_resid(x, w, h, mod3, layer, which_gate, rows_per_cond, cond_base):
    m = x.shape[0]
    tm = min(TM_MM, rows_per_cond)

    def gate_map(j, i):
        return (_mod_row(layer, cond_base + (i * tm) // rows_per_cond, which_gate), 0, j)

    return pl.pallas_call(
        _mm_resid_kernel,
        grid=(D // TN_MM, m // tm),
        in_specs=[
            pl.BlockSpec((tm, D), lambda j, i: (i, 0)),
            pl.BlockSpec((D, TN_MM), lambda j, i: (0, j)),
            pl.BlockSpec((tm, TN_MM), lambda j, i: (i, j)),
            pl.BlockSpec((None, 1, TN_MM), gate_map),
        ],
        out_specs=pl.BlockSpec((tm, TN_MM), lambda j, i: (i, j)),
        out_shape=jax.ShapeDtypeStruct((m, D), F32),
        compiler_params=_params(2),
        name="proj_residual",
    )(x, w, h, mod3)


def _attn_kernel(sink_ref, q_ref, k_ref, v_ref, kc_ref, vc_ref, o_ref, *scratch, band, seq):
    kh = pl.program_id(1)
    log2e = 1.4426950408889634
    c = DH ** -0.5 * log2e
    kc = kc_ref[...]
    vc = vc_ref[...]
    cols = GQA * WINDOW
    lane = lax.broadcasted_iota(jnp.int32, (1, cols), 1)
    sink_row = jnp.full((1, cols), sink_ref[kh * GQA + GQA - 1], F32)
    for g in range(GQA - 2, -1, -1):
        sink_row = jnp.where(lane < (g + 1) * WINDOW, sink_ref[kh * GQA + g], sink_row)
    sink_row = sink_row * log2e
    n_band = 3 * WINDOW if band else 0
    if band:
        (mask_ref,) = scratch
        key_minus_query = (lax.broadcasted_iota(jnp.int32, (n_band, cols), 0)
                           - (lax.broadcasted_iota(jnp.int32, (n_band, cols), 1) & (WINDOW - 1)))
        for t in range(3):
            valid = (key_minus_query >= (t - 1) * WINDOW) & (key_minus_query <= (t + 1) * WINDOW)
            mask_ref[t] = jnp.where(valid, 0.0, NEG)

    def block(n, carry):
        q0 = pl.multiple_of(n * WINDOW, WINDOW)
        qb = q_ref[pl.ds(q0, WINDOW), :]
        q4 = jnp.concatenate([qb[:, g * DH:(g + 1) * DH] for g in range(GQA)], axis=0)
        s_ctx = _dot_nt(kc, q4)
        m_raw = jnp.max(s_ctx, axis=0, keepdims=True)
        if band:
            ws = pl.multiple_of(jnp.clip((n - 1) * WINDOW, 0, seq - n_band), WINDOW)
            s_band = _dot_nt(k_ref[pl.ds(ws, n_band), :], q4) + mask_ref[(q0 - ws) // WINDOW]
            m_raw = jnp.maximum(m_raw, jnp.max(s_band, axis=0, keepdims=True))
        m = jnp.maximum(m_raw * c, sink_row)
        p_ctx = jnp.exp2(s_ctx * c - m)
        denom = jnp.sum(p_ctx, axis=0, keepdims=True) + jnp.exp2(sink_row - m)
        o_t = lax.dot_general(vc, p_ctx.astype(BF16), (((0,), (0,)), ((), ())), preferred_element_type=F32)
        if band:
            p_band = jnp.exp2(s_band * c - m)
            denom = denom + jnp.sum(p_band, axis=0, keepdims=True)
            o_t = o_t + lax.dot_general(v_ref[pl.ds(ws, n_band), :], p_band.astype(BF16),
                                        (((0,), (0,)), ((), ())), preferred_element_type=F32)
        o_t = o_t / denom
        for g in range(GQA):
            o_ref[pl.ds(q0, WINDOW), g * DH:(g + 1) * DH] = (
                o_t[:, g * WINDOW:(g + 1) * WINDOW].T.astype(o_ref.dtype))
        return carry

    lax.fori_loop(0, q_ref.shape[0] // WINDOW, block, 0, unroll=2)


def _attention(q_arr, kv, ctx_kv, sink, batch, band):
    qkv_q = q_arr
    qkv_kv, k_col = kv
    qkv_ctx, ck_col = ctx_kv
    seq_q = qkv_q.shape[0] // batch
    seq_k = qkv_kv.shape[0] // batch
    n_ctx = qkv_ctx.shape[0] // batch
    v_col = k_col + N_KV
    cv_col = ck_col + N_KV
    return pl.pallas_call(
        functools.partial(_attn_kernel, band=band, seq=seq_k),
        grid_spec=pltpu.PrefetchScalarGridSpec(
            num_scalar_prefetch=1,
            grid=(batch, N_KV),
            in_specs=[
                pl.BlockSpec((seq_q, GQA * DH), lambda b, k, s: (b, k)),
                pl.BlockSpec((seq_k, DH), lambda b, k, s: (b, k_col + k)),
                pl.BlockSpec((seq_k, DH), lambda b, k, s: (b, v_col + k)),
                pl.BlockSpec((n_ctx, DH), lambda b, k, s: (b, ck_col + k)),
                pl.BlockSpec((n_ctx, DH), lambda b, k, s: (b, cv_col + k)),
            ],
            out_specs=pl.BlockSpec((seq_q, GQA * DH), lambda b, k, s: (b, k)),
            scratch_shapes=[pltpu.VMEM((3, 3 * WINDOW, GQA * WINDOW), F32)] if band else [],
        ),
        out_shape=jax.ShapeDtypeStruct((qkv_q.shape[0], D), BF16),
        compiler_params=_params(2),
        name="window_attn" if band else "ctx_attn",
    )(sink, qkv_q, qkv_kv, qkv_kv, qkv_ctx, qkv_ctx)


def _conv_in_kernel(x_ref, wb_ref, wc_ref, wx_ref, b_ref, u_ref):
    x = x_ref[...]
    b_ref[...] = _dot(x, wb_ref[...]).astype(b_ref.dtype)
    u_ref[...] = (_dot(x, wc_ref[...]) * _dot(x, wx_ref[...])).astype(u_ref.dtype)


def _conv_in(a, w_in):
    m = a.shape[0]
    nt = D // TN_CONV
    o_spec = pl.BlockSpec((TM_CONV_IN, TN_CONV), lambda j, i: (i, j))
    return pl.pallas_call(
        _conv_in_kernel,
        grid=(nt, m // TM_CONV_IN),
        in_specs=[
            pl.BlockSpec((TM_CONV_IN, D), lambda j, i: (i, 0)),
            pl.BlockSpec((D, TN_CONV), lambda j, i: (0, j)),
            pl.BlockSpec((D, TN_CONV), lambda j, i: (0, nt + j)),
            pl.BlockSpec((D, TN_CONV), lambda j, i: (0, 2 * nt + j)),
        ],
        out_specs=[o_spec, o_spec],
        out_shape=[jax.ShapeDtypeStruct((m, D), BF16)] * 2,
        compiler_params=_params(2),
        name="conv_in_proj",
    )(a, w_in, w_in, w_in)


def _conv_gate_kernel(b_ref, u_ref, up_ref, un_ref, w_ref, o_ref, *, seq):
    i = pl.program_id(0)
    tm = u_ref.shape[0]
    u = u_ref[...].astype(F32)
    t0 = i * tm
    prev_row = jnp.where(lax.rem(t0, seq) == 0, 0.0, up_ref[...].astype(F32)[BF16_ROWS - 1:BF16_ROWS, :])
    next_row = jnp.where(lax.rem(t0 + tm, seq) == 0, 0.0, un_ref[...].astype(F32)[0:1, :])
    row = lax.broadcasted_iota(jnp.int32, u.shape, 0)
    u_prev = jnp.where(row == 0, prev_row, pltpu.roll(u, 1, axis=0))
    u_next = jnp.where(row == tm - 1, next_row, pltpu.roll(u, tm - 1, axis=0))
    y = u_prev * w_ref[0:1, :] + u * w_ref[1:2, :] + u_next * w_ref[2:3, :]
    o_ref[...] = (b_ref[...].astype(F32) * y).astype(o_ref.dtype)


def _conv_gate(b, u, conv_w, seq):
    m = u.shape[0]
    tm = TM_CONV
    per = tm // BF16_ROWS
    last = m // BF16_ROWS - 1
    return pl.pallas_call(
        functools.partial(_conv_gate_kernel, seq=seq),
        grid=(m // tm,),
        in_specs=[
            pl.BlockSpec((tm, D), lambda i: (i, 0)),
            pl.BlockSpec((tm, D), lambda i: (i, 0)),
            pl.BlockSpec((BF16_ROWS, D), lambda i: (jnp.maximum(i * per - 1, 0), 0)),
            pl.BlockSpec((BF16_ROWS, D), lambda i: (jnp.minimum((i + 1) * per, last), 0)),
            pl.BlockSpec((3, D), lambda i: (0, 0)),
        ],
        out_specs=pl.BlockSpec((tm, D), lambda i: (i, 0)),
        out_shape=jax.ShapeDtypeStruct((m, D), BF16),
        compiler_params=_params(1),
        name="conv_gate",
    )(b, u, u, u, conv_w)


def _route_kernel(h_ref, g_ref, sh_ref, sc_ref, wr_ref, bias_ref, f_ref, route_ref, cnt_ref):
    tm = h_ref.shape[0]
    f = _rms(h_ref[...], g_ref[...]) * (1.0 + sc_ref[...]) + sh_ref[...]
    f_ref[...] = f.astype(f_ref.dtype)
    f_hi = f.astype(BF16)
    f_lo = (f - f_hi.astype(F32)).astype(BF16)
    logits = _dot(f_hi, wr_ref[0]) + (_dot(f_lo, wr_ref[0]) + _dot(f_hi, wr_ref[1]))
    lt = logits.T[:N_EXP]
    score = jax.nn.sigmoid(lt)
    sel = score + bias_ref[...]
    sel_rows = [sel[e:e + 1] for e in range(N_EXP)]
    score_rows = [score[e:e + 1] for e in range(N_EXP)]

    grp_scores = []
    for gi in range(N_GRP):
        a, b, c, d = sel_rows[gi * EXP_PER_GRP:(gi + 1) * EXP_PER_GRP]
        hi1, lo1 = jnp.maximum(a, b), jnp.minimum(a, b)
        hi2, lo2 = jnp.maximum(c, d), jnp.minimum(c, d)
        grp_scores.append(jnp.maximum(hi1, hi2) + jnp.maximum(jnp.minimum(hi1, hi2), jnp.maximum(lo1, lo2)))
    best = grp_scores[0]
    g_idx = jnp.zeros_like(best, dtype=jnp.int32)
    for gi in range(1, N_GRP):
        better = grp_scores[gi] > best
        g_idx = jnp.where(better, gi, g_idx)
        best = jnp.where(better, grp_scores[gi], best)

    masked = [jnp.where(g_idx == e // EXP_PER_GRP, sel_rows[e], NEG) for e in range(N_EXP)]
    v1 = masked[0]
    e1 = jnp.zeros_like(g_idx)
    for e in range(1, N_EXP):
        better = masked[e] > v1
        e1 = jnp.where(better, e, e1)
        v1 = jnp.where(better, masked[e], v1)
    v2 = jnp.full_like(v1, -jnp.inf)
    e2 = jnp.zeros_like(g_idx)
    for e in range(N_EXP):
        better = (masked[e] > v2) & (e1 != e)
        e2 = jnp.where(better, e, e2)
        v2 = jnp.where(better, masked[e], v2)
    s1 = jnp.zeros_like(v1)
    s2 = jnp.zeros_like(v1)
    for e in range(N_EXP):
        s1 = jnp.where(e1 == e, score_rows[e], s1)
        s2 = jnp.where(e2 == e, score_rows[e], s2)
    tot = s1 + s2
    w1 = s1 / tot
    w2 = s2 / tot

    e_iota = lax.broadcasted_iota(jnp.int32, (N_EXP, tm), 0)
    hit = (e_iota == e1) | (e_iota == e2)
    onehot = jnp.where(hit, 1.0, 0.0).astype(BF16)
    r_i = lax.broadcasted_iota(jnp.int32, (tm, tm), 0)
    c_i = lax.broadcasted_iota(jnp.int32, (tm, tm), 1)
    upper = jnp.where(r_i <= c_i, 1.0, 0.0).astype(BF16)
    rank = _dot(onehot, upper) - 1.0
    rank1 = jnp.sum(jnp.where(e_iota == e1, rank, 0.0), axis=0, keepdims=True)
    rank2 = jnp.sum(jnp.where(e_iota == e2, rank, 0.0), axis=0, keepdims=True)
    cnt_ref[...] = jnp.broadcast_to(jnp.sum(jnp.where(hit, 1.0, 0.0), axis=1, keepdims=True), cnt_ref.shape)

    route_ref[0:1, :] = e1.astype(F32)
    route_ref[1:2, :] = e2.astype(F32)
    route_ref[2:3, :] = w1
    route_ref[3:4, :] = w2
    route_ref[4:5, :] = rank1
    route_ref[5:6, :] = rank2
    route_ref[6:8, :] = jnp.zeros((2, tm), F32)


def _route(h, g, mod3, layer, rows_per_cond, cond_base, wr_pad, bias_col):
    m = h.shape[0]
    tm = min(TM_ROW, rows_per_cond)

    def row(which):
        return lambda i: (_mod_row(layer, cond_base + (i * tm) // rows_per_cond, which), 0, 0)

    return pl.pallas_call(
        _route_kernel,
        grid=(m // tm,),
        in_specs=[
            pl.BlockSpec((tm, D), lambda i: (i, 0)),
            pl.BlockSpec((1, D), lambda i: (0, 0)),
            pl.BlockSpec((None, 1, D), row(3)),
            pl.BlockSpec((None, 1, D), row(4)),
            pl.BlockSpec((2, D, LANE), lambda i: (0, 0, 0)),
            pl.BlockSpec((N_EXP, 1), lambda i: (0, 0)),
        ],
        out_specs=[
            pl.BlockSpec((tm, D), lambda i: (i, 0)),
            pl.BlockSpec((SUBLANE, tm), lambda i: (0, i)),
            pl.BlockSpec((None, N_EXP, LANE), lambda i: (i, 0, 0)),
        ],
        out_shape=[
            jax.ShapeDtypeStruct((m, D), BF16),
            jax.ShapeDtypeStruct((SUBLANE, m), F32),
            jax.ShapeDtypeStruct((m // tm, N_EXP, LANE), F32),
        ],
        compiler_params=_params(1),
        name="norm_route",
    )(h, g.reshape(1, D), mod3, mod3, wr_pad, bias_col)


SEG_ALIGN = BF16_ROWS
SEG_SIZES = (512, 256, 128, 64, 32, 16)
TAIL_SIZES = (128, 64, 32, 16)
SORT_ROWS = 2 * TM_ROW + N_EXP * SEG_ALIGN


def _seg_copies(src, src_off, dst_hbm, dst_off, length, sem, sizes):
    out = []
    for size in sizes:
        take = (length & size) != 0
        cp = pltpu.make_async_copy(src.at[pl.ds(pl.multiple_of(src_off, SEG_ALIGN), size)],
                                   dst_hbm.at[pl.ds(pl.multiple_of(dst_off, SEG_ALIGN), size)], sem)
        out.append((take, cp))
        src_off = src_off + jnp.where(take, size, 0)
        dst_off = dst_off + jnp.where(take, size, 0)
    return out


def _run_copies(copies):
    for take, cp in copies:
        @pl.when(take)
        def _():
            cp.start()
    for take, cp in copies:
        @pl.when(take)
        def _():
            cp.wait()


def _sort_dispatch_kernel(seg_ref, tail_ref, f_ref, pos_ref, xs_hbm, sbuf, zbuf, sem, *, n_slot_tiles):
    i = pl.program_id(0)
    tm = f_ref.shape[0]
    slot = lax.broadcasted_iota(jnp.int32, (SORT_ROWS, tm), 0)
    onehot = jnp.where((slot == pos_ref[0:1, :]) | (slot == pos_ref[1:2, :]), 1.0, 0.0).astype(BF16)
    for cb in range(D // TN_MM):
        cols = slice(cb * TN_MM, (cb + 1) * TN_MM)
        sbuf[:, cols] = _dot(onehot, f_ref[:, cols]).astype(BF16)
    copies = []
    for e in range(N_EXP):
        copies += _seg_copies(sbuf, seg_ref[i, e], xs_hbm, seg_ref[i, 2 * N_EXP + e], seg_ref[i, N_EXP + e],
                              sem, SEG_SIZES)
    _run_copies(copies)

    @pl.when(i == pl.num_programs(0) - 1)
    def _():
        zbuf[...] = jnp.zeros_like(zbuf)
        tails = []
        for e in range(N_EXP):
            tails += _seg_copies(zbuf, 0, xs_hbm, tail_ref[e], tail_ref[N_EXP + e], sem, TAIL_SIZES)
        _run_copies(tails)

        def unused_tile(t):
            return pltpu.make_async_copy(zbuf, xs_hbm.at[pl.ds(pl.multiple_of(t * TM_EXP, TM_EXP), TM_EXP)], sem)

        def start(t, carry):
            unused_tile(t).start()
            return carry

        def wait(t, carry):
            unused_tile(t).wait()
            return carry

        lax.fori_loop(tail_ref[2 * N_EXP], n_slot_tiles, start, 0)
        lax.fori_loop(tail_ref[2 * N_EXP], n_slot_tiles, wait, 0)


def _sort_dispatch(f, pos, seg, tail, n_slot_tiles):
    t = f.shape[0]
    return pl.pallas_call(
        functools.partial(_sort_dispatch_kernel, n_slot_tiles=n_slot_tiles),
        grid_spec=pltpu.PrefetchScalarGridSpec(
            num_scalar_prefetch=2,
            grid=(t // TM_ROW,),
            in_specs=[
                pl.BlockSpec((TM_ROW, D), lambda i, sg, tl: (i, 0)),
                pl.BlockSpec((SUBLANE, TM_ROW), lambda i, sg, tl: (0, i)),
            ],
            out_specs=pl.BlockSpec(memory_space=pl.ANY),
            scratch_shapes=[pltpu.VMEM((SORT_ROWS, D), BF16), pltpu.VMEM((TM_EXP, D), BF16),
                            pltpu.SemaphoreType.DMA(())],
        ),
        out_shape=jax.ShapeDtypeStruct((n_slot_tiles * TM_EXP, D), BF16),
        compiler_params=_params(1),
        name="moe_sort_dispatch",
    )(seg, tail, f, pos)


(ST_EXPERT, ST_WCOL, ST_XROW, ST_OROW, ST_OCOL, ST_VALID, ST_FIRST, ST_SLOT, ST_NEXT_E, ST_NEXT_WCOL,
 ST_HAS_NEXT) = range(11)


def _gate_up_weights(w_hbm, wbuf, sem, layer, e, wcol, slot):
    nf = D_FF // TF_EXP
    col_g = pl.multiple_of(wcol * TF_EXP, TF_EXP)
    col_u = pl.multiple_of((nf + wcol) * TF_EXP, TF_EXP)
    return (pltpu.make_async_copy(w_hbm.at[layer, e, :, pl.ds(col_g, TF_EXP)], wbuf.at[slot, 0], sem.at[slot, 0]),
            pltpu.make_async_copy(w_hbm.at[layer, e, :, pl.ds(col_u, TF_EXP)], wbuf.at[slot, 1], sem.at[slot, 1]))


def _gate_up_kernel(st_ref, x_ref, w_hbm, o_ref, wbuf, sem, *, layer):
    s = pl.program_id(0)
    valid = st_ref[ST_VALID, s] > 0
    slot = st_ref[ST_SLOT, s]

    @pl.when(valid & (st_ref[ST_FIRST, s] > 0))
    def _():
        cur = _gate_up_weights(w_hbm, wbuf, sem, layer, st_ref[ST_EXPERT, s], st_ref[ST_WCOL, s], slot)

        @pl.when(s == 0)
        def _():
            for c in cur:
                c.start(priority=WEIGHT_QUEUE)

        for c in cur:
            c.wait()

        @pl.when(st_ref[ST_HAS_NEXT, s] > 0)
        def _():
            for c in _gate_up_weights(w_hbm, wbuf, sem, layer, st_ref[ST_NEXT_E, s], st_ref[ST_NEXT_WCOL, s],
                                      1 - slot):
                c.start(priority=WEIGHT_QUEUE)

    @pl.when(valid)
    def _():
        x = x_ref[...]
        gate = _dot(x, wbuf[slot, 0])
        up = _dot(x, wbuf[slot, 1])
        o_ref[...] = (gate * jax.nn.sigmoid(gate) * up).astype(o_ref.dtype)

    @pl.when(jnp.logical_not(valid))
    def _():
        o_ref[...] = jnp.zeros_like(o_ref)


def _gate_up(xs, w_gate_up, layer, steps):
    n_steps = steps.shape[1]
    return pl.pallas_call(
        functools.partial(_gate_up_kernel, layer=layer),
        grid_spec=pltpu.PrefetchScalarGridSpec(
            num_scalar_prefetch=1,
            grid=(n_steps,),
            in_specs=[
                pl.BlockSpec((TM_EXP, D), lambda s, st: (st[ST_XROW, s], 0)),
                pl.BlockSpec(memory_space=pl.ANY),
            ],
            out_specs=pl.BlockSpec((TM_EXP, TF_EXP), lambda s, st: (st[ST_OROW, s], st[ST_OCOL, s])),
            scratch_shapes=[pltpu.VMEM((2, 2, D, TF_EXP), F32), pltpu.SemaphoreType.DMA((2, 2))],
        ),
        out_shape=jax.ShapeDtypeStruct((xs.shape[0], D_FF), BF16),
        compiler_params=_params(1),
        name="moe_gate_up",
    )(steps, xs, w_gate_up)


TL_EXPERT, TL_SRC, TL_FIRST, TL_SLOT, TL_NEXT_E, TL_HAS_NEXT = range(6)


def _down_weights(w_hbm, wbuf, sem, layer, e, slot):
    return pltpu.make_async_copy(w_hbm.at[layer, e], wbuf.at[slot], sem.at[slot])


def _down_kernel(tl_ref, a_ref, w_hbm, o_ref, wbuf, sem, *, layer):
    i = pl.program_id(0)
    valid = tl_ref[TL_SRC, i] == i
    slot = tl_ref[TL_SLOT, i]

    @pl.when(valid & (tl_ref[TL_FIRST, i] > 0))
    def _():
        cur = _down_weights(w_hbm, wbuf, sem, layer, tl_ref[TL_EXPERT, i], slot)

        @pl.when(i == 0)
        def _():
            cur.start(priority=WEIGHT_QUEUE)

        cur.wait()

        @pl.when(tl_ref[TL_HAS_NEXT, i] > 0)
        def _():
            _down_weights(w_hbm, wbuf, sem, layer, tl_ref[TL_NEXT_E, i], 1 - slot).start(priority=WEIGHT_QUEUE)

    @pl.when(valid)
    def _():
        o_ref[...] = _dot(a_ref[...], wbuf[slot]).astype(o_ref.dtype)

    @pl.when(jnp.logical_not(valid))
    def _():
        o_ref[...] = jnp.zeros_like(o_ref)


def _down(act, w_down, layer, tiles):
    nt = tiles.shape[1]
    return pl.pallas_call(
        functools.partial(_down_kernel, layer=layer),
        grid_spec=pltpu.PrefetchScalarGridSpec(
            num_scalar_prefetch=1,
            grid=(nt,),
            in_specs=[
                pl.BlockSpec((TM_EXP, D_FF), lambda i, tl: (tl[TL_SRC, i], 0)),
                pl.BlockSpec(memory_space=pl.ANY),
            ],
            out_specs=pl.BlockSpec((TM_EXP, D), lambda i, tl: (i, 0)),
            scratch_shapes=[pltpu.VMEM((2, D_FF, D), F32), pltpu.SemaphoreType.DMA((2,))],
        ),
        out_shape=jax.ShapeDtypeStruct((act.shape[0], D), BF16),
        compiler_params=_params(1),
        name="moe_down",
    )(tiles, act, w_down)


def _combine_kernel(seg_ref, y_hbm, pos_ref, route_ref, h_hbm, gate_ref, *rest, tail):
    if tail == "next":
        ng_ref, nsh_ref, nsc_ref, o_ref, a_ref, ybuf, hbuf, sem, hsem = rest
    elif tail == "final":
        ng_ref, o_ref, ybuf, hbuf, sem, hsem = rest
    else:
        o_ref, ybuf, hbuf, sem, hsem = rest
    i = pl.program_id(0)
    tm = o_ref.shape[0]
    h_copy = pltpu.make_async_copy(h_hbm.at[pl.ds(pl.multiple_of(i * tm, tm), tm)], hbuf, hsem)
    h_copy.start()

    def run_copies(tile):
        copies = []
        for e in range(N_EXP):
            copies += _seg_copies(y_hbm, seg_ref[tile, 2 * N_EXP + e], ybuf, seg_ref[tile, e],
                                  seg_ref[tile, N_EXP + e], sem, SEG_SIZES)
        return copies

    def start(copies):
        for take, cp in copies:
            @pl.when(take)
            def _():
                cp.start()

    @pl.when(i == 0)
    def _():
        ybuf[...] = jnp.zeros_like(ybuf)
        start(run_copies(0))

    for take, cp in run_copies(i):
        @pl.when(take)
        def _():
            cp.wait()

    slot = lax.broadcasted_iota(jnp.int32, (SORT_ROWS, tm), 0)
    pick = jnp.where(slot == pos_ref[0:1, :], route_ref[2:3, :],
                     jnp.where(slot == pos_ref[1:2, :], route_ref[3:4, :], 0.0)).astype(BF16)
    for cb in range(D // TN_MM):
        cols = slice(cb * TN_MM, (cb + 1) * TN_MM)
        o_ref[:, cols] = lax.dot_general(pick, ybuf[:, cols], (((0,), (0,)), ((), ())),
                                         preferred_element_type=F32)

    @pl.when(i + 1 < pl.num_programs(0))
    def _():
        start(run_copies(jnp.minimum(i + 1, pl.num_programs(0) - 1)))

    h_copy.wait()
    out = hbuf[...] + gate_ref[...] * o_ref[...]
    if tail == "final":
        out = _rms(out, ng_ref[...])
    o_ref[...] = out
    if tail == "next":
        a_ref[...] = (_rms(out, ng_ref[...]) * (1.0 + nsc_ref[...]) + nsh_ref[...]).astype(a_ref.dtype)


def _combine(y, seg, pos, route, h, mod3, layer, rows_per_cond, cond_base, tail, tail_g):
    m = h.shape[0]
    tm = TM_ROW

    def mod_spec(lyr, which):
        return pl.BlockSpec((None, 1, D),
                            lambda i, sg: (_mod_row(lyr, cond_base + (i * tm) // rows_per_cond, which), 0, 0))

    row_spec = pl.BlockSpec((tm, D), lambda i, sg: (i, 0))
    tok_spec = pl.BlockSpec((SUBLANE, tm), lambda i, sg: (0, i))
    in_specs = [pl.BlockSpec(memory_space=pl.ANY), tok_spec, tok_spec, pl.BlockSpec(memory_space=pl.ANY),
                mod_spec(layer, 5)]
    args = [seg, y, pos, route, h, mod3]
    out_specs, out_shape = row_spec, jax.ShapeDtypeStruct((m, D), F32)
    if tail is not None:
        in_specs.append(pl.BlockSpec((1, D), lambda i, sg: (0, 0)))
        args.append(tail_g.reshape(1, D))
    if tail == "next":
        in_specs += [mod_spec(layer + 1, 0), mod_spec(layer + 1, 1)]
        args += [mod3, mod3]
        out_specs = [row_spec, row_spec]
        out_shape = [out_shape, jax.ShapeDtypeStruct((m, D), BF16)]
    return pl.pallas_call(
        functools.partial(_combine_kernel, tail=tail),
        grid_spec=pltpu.PrefetchScalarGridSpec(
            num_scalar_prefetch=1,
            grid=(m // tm,),
            in_specs=in_specs,
            out_specs=out_specs,
            scratch_shapes=[pltpu.VMEM((SORT_ROWS, D), BF16), pltpu.VMEM((tm, D), F32),
                            pltpu.SemaphoreType.DMA(()), pltpu.SemaphoreType.DMA(())],
        ),
        out_shape=out_shape,
        compiler_params=_params(1),
        name="moe_combine",
    )(*args)


def _moe_block(h, norm_g, mod3, layer, rows_per_cond, cond_base, wr_pad, bias_col, w_gate_up, w_down,
               tail=None, tail_g=None):
    t = h.shape[0]
    f, route, cnt = _route(h, norm_g, mod3, layer, rows_per_cond, cond_base, wr_pad, bias_col)

    i32 = jnp.int32
    e_ids = jnp.arange(N_EXP, dtype=i32)

    def take(table, idx):
        return jnp.sum(jnp.where(idx[:, None] == e_ids[None, :], table[None, :], 0), axis=1).astype(i32)

    def bucket(ends, pos):
        return jnp.sum((ends[None, :] <= pos[:, None]).astype(i32), axis=1)

    n_tok_tiles = t // TM_ROW
    cnt_tile = cnt[:, :, 0].astype(i32)
    seg_len = (cnt_tile + SEG_ALIGN - 1) // SEG_ALIGN * SEG_ALIGN
    rows_e = jnp.sum(seg_len, axis=0)
    tiles_per_e = (rows_e + TM_EXP - 1) // TM_EXP
    tile_end = jnp.cumsum(tiles_per_e).astype(i32)
    tile_start = tile_end - tiles_per_e
    row_start = tile_start * TM_EXP
    seg_out = row_start[None, :] + jnp.cumsum(seg_len, axis=0) - seg_len
    seg_in = jnp.cumsum(seg_len, axis=1) - seg_len
    seg = jnp.concatenate([seg_in, seg_len, seg_out], axis=1).astype(i32)

    def take_tile(table, e):
        e_tiles = e.reshape(n_tok_tiles, TM_ROW)
        hit = e_tiles[:, :, None] == e_ids[None, None, :]
        return jnp.sum(jnp.where(hit, table[:, None, :], 0), axis=2).reshape(t).astype(i32)

    e1 = route[0].astype(i32)
    e2 = route[1].astype(i32)
    r1 = route[4].astype(i32)
    r2 = route[5].astype(i32)
    pos = jnp.zeros((SUBLANE, t), i32).at[0].set(take_tile(seg_in, e1) + r1).at[1].set(take_tile(seg_in, e2) + r2)
    nt = (2 * t + n_tok_tiles * N_EXP * SEG_ALIGN) // TM_EXP + N_EXP
    n_valid = jnp.maximum(tile_end[-1], 1)
    tail_tab = jnp.concatenate([row_start + rows_e, tiles_per_e * TM_EXP - rows_e, tile_end[-1:]]).astype(i32)

    nonempty = tiles_per_e > 0
    order = (jnp.cumsum(nonempty.astype(i32)) - nonempty.astype(i32)).astype(i32)
    later = nonempty[None, :] & (e_ids[None, :] > e_ids[:, None])
    next_e = jnp.min(jnp.where(later, e_ids[None, :], N_EXP), axis=1).astype(i32)

    tile_id = jnp.arange(nt, dtype=i32)
    tile_src = jnp.minimum(tile_id, n_valid - 1)
    tile_e = jnp.minimum(bucket(tile_end, tile_src), N_EXP - 1)
    tile_next = take(next_e, tile_e)
    tiles = jnp.stack([
        tile_e, tile_src,
        (tile_src == take(tile_start, tile_e)).astype(i32),
        take(order, tile_e) % 2,
        jnp.minimum(tile_next, N_EXP - 1),
        (tile_next < N_EXP).astype(i32),
    ]).astype(i32)

    nf = D_FF // TF_EXP
    step_id = jnp.arange(nf * nt, dtype=i32)
    step_valid = step_id < nf * n_valid
    sid = jnp.minimum(step_id, nf * n_valid - 1)
    step_e = jnp.minimum(bucket(nf * tile_end, sid), N_EXP - 1)
    local = sid - nf * take(tile_start, step_e)
    n_e = jnp.maximum(take(tiles_per_e, step_e), 1)
    step_f = local // n_e
    step_r = take(tile_start, step_e) + local % n_e
    pad = step_id - nf * n_valid
    last_col = step_f == nf - 1
    step_next_e = jnp.where(last_col, take(next_e, step_e), step_e)
    steps = jnp.stack([
        step_e, step_f, step_r,
        jnp.where(step_valid, step_r, n_valid + pad // nf),
        jnp.where(step_valid, step_f, pad % nf),
        step_valid.astype(i32),
        (local % n_e == 0).astype(i32),
        (nf * take(order, step_e) + step_f) % 2,
        jnp.minimum(step_next_e, N_EXP - 1),
        jnp.where(last_col, 0, step_f + 1),
        (step_next_e < N_EXP).astype(i32),
    ]).astype(i32)

    xs = _sort_dispatch(f, pos, seg, tail_tab, nt)
    act = _gate_up(xs, w_gate_up, layer, steps)
    y = _down(act, w_down, layer, tiles)

    return _combine(y, seg, pos, route, h, mod3, layer, rows_per_cond, cond_base, tail, tail_g)


def _rope_tables(seq):
    rows = seq // GRID_W
    row = jnp.repeat(jnp.arange(rows), GRID_W).astype(F32)
    col = jnp.tile(jnp.arange(GRID_W), rows).astype(F32)
    n_freq = DH // 4
    inv_freq = ROPE_THETA ** (-jnp.arange(n_freq, dtype=F32) / n_freq)
    ang = jnp.concatenate([row[:, None] * inv_freq, col[:, None] * inv_freq], axis=-1)
    cos, sin = jnp.cos(ang), jnp.sin(ang)
    return jnp.concatenate([cos, cos], axis=-1), jnp.concatenate([-sin, sin], axis=-1)


def kernel(x, c, ctx, c_ctx, w_ada, b_ada, norm1_g, norm2_g, attn_w_qkv, attn_w_o, attn_sink, conv_w_in, conv_w,
           conv_w_out, w_router, router_bias, moe_w_gate_up, moe_w_down, final_g):
    batch, seq, _ = x.shape
    n_ctx = ctx.shape[1]
    depth = w_ada.shape[0]
    assert depth == 2 and x.shape[2] == D
    ctx_row = batch

    c8 = jnp.concatenate([c, c_ctx[None, :], jnp.zeros((SUBLANE - batch - 1, D), F32)], axis=0)
    mod3 = _ada(c8, w_ada, b_ada).reshape(depth * SUBLANE * N_MOD, 1, D)
    wr_f32 = jnp.pad(w_router, ((0, 0), (0, LANE - N_EXP)))
    wr_hi = wr_f32.astype(BF16)
    wr_pad = jnp.stack([wr_hi, (wr_f32 - wr_hi.astype(F32)).astype(BF16)])
    bias_col = router_bias.astype(F32).reshape(N_EXP, 1)
    rope = _rope_tables(seq)

    h_lat = x.reshape(batch * seq, D)
    h_ctx = ctx.reshape(batch * n_ctx, D)

    a_lat = _normmod(h_lat, norm1_g[0], mod3, 0, 0, seq, 0)
    a_ctx = _normmod(h_ctx, norm1_g[0], mod3, 0, 0, batch * n_ctx, ctx_row)
    qkv_lat = _qkv_proj(a_lat, attn_w_qkv[0], rope)
    q_tiles = N_HEADS * DH // TN_MM
    q_ctx = _qkv_proj(a_ctx, attn_w_qkv[0], col_tiles=(0, q_tiles))
    kv_ctx = _qkv_proj(a_ctx, attn_w_qkv[0], col_tiles=(q_tiles, QKV // TN_MM - q_tiles))
    o_lat = _attention(qkv_lat, (qkv_lat, N_HEADS), (kv_ctx, 0), attn_sink[0], batch, band=True)
    o_ctx = _attention(q_ctx, (kv_ctx, 0), (kv_ctx, 0), attn_sink[0], batch, band=False)
    h_lat = _proj_resid(o_lat, attn_w_o[0], h_lat, mod3, 0, 2, seq, 0)
    h_ctx = _proj_resid(o_ctx, attn_w_o[0], h_ctx, mod3, 0, 2, batch * n_ctx, ctx_row)
    moe_args = (wr_pad, bias_col, moe_w_gate_up, moe_w_down)
    h_lat, a_lat = _moe_block(h_lat, norm2_g[0], mod3, 0, seq, 0, *moe_args, "next", norm1_g[1])
    h_ctx, a_ctx = _moe_block(h_ctx, norm2_g[0], mod3, 0, batch * n_ctx, ctx_row, *moe_args, "next", norm1_g[1])

    b_gate, u = _conv_in(a_lat, conv_w_in[0])
    z = _conv_gate(b_gate, u, conv_w[0], seq)
    h_lat = _proj_resid(z, conv_w_out[0], h_lat, mod3, 1, 2, seq, 0)
    h_lat = _moe_block(h_lat, norm2_g[1], mod3, 1, seq, 0, *moe_args, "final", final_g)
    return h_lat.reshape(batch, seq, D)
```

```python
import functools

import jax
import jax.numpy as jnp
from jax import lax
from jax.experimental import pallas as pl
from jax.experimental.pallas import tpu as pltpu

D = 4096
N_HEADS = 32
N_KV = 8
GQA = 4
DH = 128
QKV = (N_HEADS + 2 * N_KV) * DH
GRID_W = 64
WINDOW = 128
ROPE_THETA = 10000.0
N_EXP = 16
N_GRP = 4
EXP_PER_GRP = 4
D_FF = 1024
N_MOD = 6
EPS = 1e-6
NEG = -1e30

LANE = 128
SUBLANE = 8
BF16_ROWS = 2 * SUBLANE
VMEM_LIMIT = 56 * 1024 * 1024
WEIGHT_QUEUE = 1

TM_MM = 512
TN_MM = 1024
TN_CONV = 256
TM_CONV_IN = 1024
TM_ROW = 512
TM_CONV = 256
TM_EXP = 256
TF_EXP = 512

F32 = jnp.float32
BF16 = jnp.bfloat16


def _params(n_axes):
    return pltpu.CompilerParams(dimension_semantics=("arbitrary",) * n_axes,
                                vmem_limit_bytes=VMEM_LIMIT)


def _dot(a, b):
    return lax.dot_general(a, b, (((1,), (0,)), ((), ())), preferred_element_type=F32)


def _dot_nt(a, b):
    return lax.dot_general(a, b, (((1,), (1,)), ((), ())), preferred_element_type=F32)


def _ada_kernel(c_ref, w_ref, b_ref, o_ref):
    s = c_ref[...]
    s = (s * jax.nn.sigmoid(s)).astype(BF16)
    o_ref[...] = _dot(s, w_ref[...]) + b_ref[...]


def _ada(c8, w_ada, b_ada):
    depth = w_ada.shape[0]
    n = w_ada.shape[2]
    return pl.pallas_call(
        _ada_kernel,
        grid=(depth, n // TN_MM),
        in_specs=[
            pl.BlockSpec((SUBLANE, D), lambda l, j: (0, 0)),
            pl.BlockSpec((None, D, TN_MM), lambda l, j: (l, 0, j)),
            pl.BlockSpec((None, 1, TN_MM), lambda l, j: (l, 0, j)),
        ],
        out_specs=pl.BlockSpec((None, SUBLANE, TN_MM), lambda l, j: (l, 0, j)),
        out_shape=jax.ShapeDtypeStruct((depth, SUBLANE, n), F32),
        compiler_params=_params(2),
        name="ada_mod",
    )(c8, w_ada, b_ada.reshape(depth, 1, n))


def _mod_row(layer, cond_row, which):
    return (layer * SUBLANE + cond_row) * N_MOD + which


def _rms(x, g):
    return x * lax.rsqrt(jnp.mean(x * x, axis=-1, keepdims=True) + EPS) * g


def _normmod_kernel(h_ref, g_ref, sh_ref, sc_ref, o_ref):
    y = _rms(h_ref[...], g_ref[...])
    o_ref[...] = (y * (1.0 + sc_ref[...]) + sh_ref[...]).astype(o_ref.dtype)


def _normmod(h, g, mod3, layer, which_shift, rows_per_cond, cond_base):
    m = h.shape[0]
    tm = min(TM_ROW, rows_per_cond)

    def row(which):
        return lambda i: (_mod_row(layer, cond_base + (i * tm) // rows_per_cond, which), 0, 0)

    return pl.pallas_call(
        _normmod_kernel,
        grid=(m // tm,),
        in_specs=[
            pl.BlockSpec((tm, D), lambda i: (i, 0)),
            pl.BlockSpec((1, D), lambda i: (0, 0)),
            pl.BlockSpec((None, 1, D), row(which_shift)),
            pl.BlockSpec((None, 1, D), row(which_shift + 1)),
        ],
        out_specs=pl.BlockSpec((tm, D), lambda i: (i, 0)),
        out_shape=jax.ShapeDtypeStruct((m, D), BF16),
        compiler_params=_params(1),
        name="norm_modulate",
    )(h, g.reshape(1, D), mod3, mod3)


def _mm_plain_kernel(x_ref, w_ref, o_ref):
    o_ref[...] = _dot(x_ref[...], w_ref[...]).astype(o_ref.dtype)


def _mm_rope_kernel(x_ref, w_ref, cos_ref, sin_ref, o_ref, *, n_rope_tiles):
    acc = _dot(x_ref[...], w_ref[...])

    @pl.when(pl.program_id(0) < n_rope_tiles)
    def _():
        cosf = cos_ref[...]
        sinf = sin_ref[...]
        for hh in range(acc.shape[1] // DH):
            xh = acc[:, hh * DH:(hh + 1) * DH]
            o_ref[:, hh * DH:(hh + 1) * DH] = (
                xh * cosf + pltpu.roll(xh, DH // 2, axis=1) * sinf).astype(o_ref.dtype)

    @pl.when(pl.program_id(0) >= n_rope_tiles)
    def _():
        o_ref[...] = acc.astype(o_ref.dtype)


def _mm_resid_kernel(x_ref, w_ref, h_ref, gate_ref, o_ref):
    o_ref[...] = h_ref[...] + gate_ref[...] * _dot(x_ref[...], w_ref[...])


def _qkv_proj(a, w_qkv, rope=None, col_tiles=(0, QKV // TN_MM)):
    m = a.shape[0]
    first, count = col_tiles
    grid = (count, m // TM_MM)
    x_spec = pl.BlockSpec((TM_MM, D), lambda j, i: (i, 0))
    w_spec = pl.BlockSpec((D, TN_MM), lambda j, i: (0, first + j))
    o_spec = pl.BlockSpec((TM_MM, TN_MM), lambda j, i: (i, j))
    out_shape = jax.ShapeDtypeStruct((m, count * TN_MM), BF16)
    if rope is None:
        return pl.pallas_call(_mm_plain_kernel, grid=grid, in_specs=[x_spec, w_spec], out_specs=o_spec,
                              out_shape=out_shape, compiler_params=_params(2), name="qkv_ctx")(a, w_qkv)
    assert col_tiles == (0, QKV // TN_MM)
    cosf, sinf = rope
    seq_tiles = cosf.shape[0] // TM_MM
    t_spec = pl.BlockSpec((TM_MM, DH), lambda j, i: (i % seq_tiles, 0))
    n_rope_tiles = (N_HEADS + N_KV) * DH // TN_MM
    return pl.pallas_call(
        functools.partial(_mm_rope_kernel, n_rope_tiles=n_rope_tiles),
        grid=grid, in_specs=[x_spec, w_spec, t_spec, t_spec], out_specs=o_spec,
        out_shape=out_shape, compiler_params=_params(2), name="qkv_rope")(a, w_qkv, cosf, sinf)


def _proj_resid(x, w, h, mod3, layer, which_gate, rows_per_cond, cond_base):
    m = x.shape[0]
    tm = min(TM_MM, rows_per_cond)

    def gate_map(j, i):
        return (_mod_row(layer, cond_base + (i * tm) // rows_per_cond, which_gate), 0, j)

    return pl.pallas_call(
        _mm_resid_kernel,
        grid=(D // TN_MM, m // tm),
        in_specs=[
            pl.BlockSpec((tm, D), lambda j, i: (i, 0)),
            pl.BlockSpec((D, TN_MM), lambda j, i: (0, j)),
            pl.BlockSpec((tm, TN_MM), lambda j, i: (i, j)),
            pl.BlockSpec((None, 1, TN_MM), gate_map),
        ],
        out_specs=pl.BlockSpec((tm, TN_MM), lambda j, i: (i, j)),
        out_shape=jax.ShapeDtypeStruct((m, D), F32),
        compiler_params=_params(2),
        name="proj_residual",
    )(x, w, h, mod3)


def _attn_kernel(sink_ref, q_ref, k_ref, v_ref, kc_ref, vc_ref, o_ref, *scratch, band, seq):
    kh = pl.program_id(1)
    log2e = 1.4426950408889634
    c = DH ** -0.5 * log2e
    kc = kc_ref[...]
    vc = vc_ref[...]
    cols = GQA * WINDOW
    lane = lax.broadcasted_iota(jnp.int32, (1, cols), 1)
    sink_row = jnp.full((1, cols), sink_ref[kh * GQA + GQA - 1], F32)
    for g in range(GQA - 2, -1, -1):
        sink_row = jnp.where(lane < (g + 1) * WINDOW, sink_ref[kh * GQA + g], sink_row)
    sink_row = sink_row * log2e
    n_band = 3 * WINDOW if band else 0
    if band:
        (mask_ref,) = scratch
        key_minus_query = (lax.broadcasted_iota(jnp.int32, (n_band, cols), 0)
                           - (lax.broadcasted_iota(jnp.int32, (n_band, cols), 1) & (WINDOW - 1)))
        for t in range(3):
            valid = (key_minus_query >= (t - 1) * WINDOW) & (key_minus_query <= (t + 1) * WINDOW)
            mask_ref[t] = jnp.where(valid, 0.0, NEG)

    def block(n, carry):
        q0 = pl.multiple_of(n * WINDOW, WINDOW)
        qb = q_ref[pl.ds(q0, WINDOW), :]
        q4 = jnp.concatenate([qb[:, g * DH:(g + 1) * DH] for g in range(GQA)], axis=0)
        s_ctx = _dot_nt(kc, q4)
        m_raw = jnp.max(s_ctx, axis=0, keepdims=True)
        if band:
            ws = pl.multiple_of(jnp.clip((n - 1) * WINDOW, 0, seq - n_band), WINDOW)
            s_band = _dot_nt(k_ref[pl.ds(ws, n_band), :], q4) + mask_ref[(q0 - ws) // WINDOW]
            m_raw = jnp.maximum(m_raw, jnp.max(s_band, axis=0, keepdims=True))
        m = jnp.maximum(m_raw * c, sink_row)
        p_ctx = jnp.exp2(s_ctx * c - m)
        denom = jnp.sum(p_ctx, axis=0, keepdims=True) + jnp.exp2(sink_row - m)
        o_t = lax.dot_general(vc, p_ctx.astype(BF16), (((0,), (0,)), ((), ())), preferred_element_type=F32)
        if band:
            p_band = jnp.exp2(s_band * c - m)
            denom = denom + jnp.sum(p_band, axis=0, keepdims=True)
            o_t = o_t + lax.dot_general(v_ref[pl.ds(ws, n_band), :], p_band.astype(BF16),
                                        (((0,), (0,)), ((), ())), preferred_element_type=F32)
        o_t = o_t / denom
        for g in range(GQA):
            o_ref[pl.ds(q0, WINDOW), g * DH:(g + 1) * DH] = (
                o_t[:, g * WINDOW:(g + 1) * WINDOW].T.astype(o_ref.dtype))
        return carry

    lax.fori_loop(0, q_ref.shape[0] // WINDOW, block, 0, unroll=8)


def _attention(q_arr, kv, ctx_kv, sink, batch, band):
    qkv_q = q_arr
    qkv_kv, k_col = kv
    qkv_ctx, ck_col = ctx_kv
    seq_q = qkv_q.shape[0] // batch
    seq_k = qkv_kv.shape[0] // batch
    n_ctx = qkv_ctx.shape[0] // batch
    v_col = k_col + N_KV
    cv_col = ck_col + N_KV
    return pl.pallas_call(
        functools.partial(_attn_kernel, band=band, seq=seq_k),
        grid_spec=pltpu.PrefetchScalarGridSpec(
            num_scalar_prefetch=1,
            grid=(batch, N_KV),
            in_specs=[
                pl.BlockSpec((seq_q, GQA * DH), lambda b, k, s: (b, k)),
                pl.BlockSpec((seq_k, DH), lambda b, k, s: (b, k_col + k)),
                pl.BlockSpec((seq_k, DH), lambda b, k, s: (b, v_col + k)),
                pl.BlockSpec((n_ctx, DH), lambda b, k, s: (b, ck_col + k)),
                pl.BlockSpec((n_ctx, DH), lambda b, k, s: (b, cv_col + k)),
            ],
            out_specs=pl.BlockSpec((seq_q, GQA * DH), lambda b, k, s: (b, k)),
            scratch_shapes=[pltpu.VMEM((3, 3 * WINDOW, GQA * WINDOW), F32)] if band else [],
        ),
        out_shape=jax.ShapeDtypeStruct((qkv_q.shape[0], D), BF16),
        compiler_params=_params(2),
        name="window_attn" if band else "ctx_attn",
    )(sink, qkv_q, qkv_kv, qkv_kv, qkv_ctx, qkv_ctx)


def _conv_in_kernel(x_ref, wb_ref, wc_ref, wx_ref, b_ref, u_ref):
    x = x_ref[...]
    b_ref[...] = _dot(x, wb_ref[...]).astype(b_ref.dtype)
    u_ref[...] = (_dot(x, wc_ref[...]) * _dot(x, wx_ref[...])).astype(u_ref.dtype)


def _conv_in(a, w_in):
    m = a.shape[0]
    nt = D // TN_CONV
    o_spec = pl.BlockSpec((TM_CONV_IN, TN_CONV), lambda j, i: (i, j))
    return pl.pallas_call(
        _conv_in_kernel,
        grid=(nt, m // TM_CONV_IN),
        in_specs=[
            pl.BlockSpec((TM_CONV_IN, D), lambda j, i: (i, 0)),
            pl.BlockSpec((D, TN_CONV), lambda j, i: (0, j)),
            pl.BlockSpec((D, TN_CONV), lambda j, i: (0, nt + j)),
            pl.BlockSpec((D, TN_CONV), lambda j, i: (0, 2 * nt + j)),
        ],
        out_specs=[o_spec, o_spec],
        out_shape=[jax.ShapeDtypeStruct((m, D), BF16)] * 2,
        compiler_params=_params(2),
        name="conv_in_proj",
    )(a, w_in, w_in, w_in)


def _conv_gate_kernel(b_ref, u_ref, up_ref, un_ref, w_ref, o_ref, *, seq):
    i = pl.program_id(0)
    tm = u_ref.shape[0]
    u = u_ref[...].astype(F32)
    t0 = i * tm
    prev_row = jnp.where(lax.rem(t0, seq) == 0, 0.0, up_ref[...].astype(F32)[BF16_ROWS - 1:BF16_ROWS, :])
    next_row = jnp.where(lax.rem(t0 + tm, seq) == 0, 0.0, un_ref[...].astype(F32)[0:1, :])
    row = lax.broadcasted_iota(jnp.int32, u.shape, 0)
    u_prev = jnp.where(row == 0, prev_row, pltpu.roll(u, 1, axis=0))
    u_next = jnp.where(row == tm - 1, next_row, pltpu.roll(u, tm - 1, axis=0))
    y = u_prev * w_ref[0:1, :] + u * w_ref[1:2, :] + u_next * w_ref[2:3, :]
    o_ref[...] = (b_ref[...].astype(F32) * y).astype(o_ref.dtype)


def _conv_gate(b, u, conv_w, seq):
    m = u.shape[0]
    tm = TM_CONV
    per = tm // BF16_ROWS
    last = m // BF16_ROWS - 1
    return pl.pallas_call(
        functools.partial(_conv_gate_kernel, seq=seq),
        grid=(m // tm,),
        in_specs=[
            pl.BlockSpec((tm, D), lambda i: (i, 0)),
            pl.BlockSpec((tm, D), lambda i: (i, 0)),
            pl.BlockSpec((BF16_ROWS, D), lambda i: (jnp.maximum(i * per - 1, 0), 0)),
            pl.BlockSpec((BF16_ROWS, D), lambda i: (jnp.minimum((i + 1) * per, last), 0)),
            pl.BlockSpec((3, D), lambda i: (0, 0)),
        ],
        out_specs=pl.BlockSpec((tm, D), lambda i: (i, 0)),
        out_shape=jax.ShapeDtypeStruct((m, D), BF16),
        compiler_params=_params(1),
        name="conv_gate",
    )(b, u, u, u, conv_w)


def _route_kernel(h_ref, g_ref, sh_ref, sc_ref, wr_ref, bias_ref, f_ref, route_ref, cnt_ref):
    tm = h_ref.shape[0]
    f = _rms(h_ref[...], g_ref[...]) * (1.0 + sc_ref[...]) + sh_ref[...]
    f_ref[...] = f.astype(f_ref.dtype)
    f_hi = f.astype(BF16)
    f_lo = (f - f_hi.astype(F32)).astype(BF16)
    logits = _dot(f_hi, wr_ref[0]) + (_dot(f_lo, wr_ref[0]) + _dot(f_hi, wr_ref[1]))
    lt = logits.T[:N_EXP]
    score = jax.nn.sigmoid(lt)
    sel = score + bias_ref[...]
    sel_rows = [sel[e:e + 1] for e in range(N_EXP)]
    score_rows = [score[e:e + 1] for e in range(N_EXP)]

    grp_scores = []
    for gi in range(N_GRP):
        a, b, c, d = sel_rows[gi * EXP_PER_GRP:(gi + 1) * EXP_PER_GRP]
        hi1, lo1 = jnp.maximum(a, b), jnp.minimum(a, b)
        hi2, lo2 = jnp.maximum(c, d), jnp.minimum(c, d)
        grp_scores.append(jnp.maximum(hi1, hi2) + jnp.maximum(jnp.minimum(hi1, hi2), jnp.maximum(lo1, lo2)))
    best = grp_scores[0]
    g_idx = jnp.zeros_like(best, dtype=jnp.int32)
    for gi in range(1, N_GRP):
        better = grp_scores[gi] > best
        g_idx = jnp.where(better, gi, g_idx)
        best = jnp.where(better, grp_scores[gi], best)

    masked = [jnp.where(g_idx == e // EXP_PER_GRP, sel_rows[e], NEG) for e in range(N_EXP)]
    v1 = masked[0]
    e1 = jnp.zeros_like(g_idx)
    for e in range(1, N_EXP):
        better = masked[e] > v1
        e1 = jnp.where(better, e, e1)
        v1 = jnp.where(better, masked[e], v1)
    v2 = jnp.full_like(v1, -jnp.inf)
    e2 = jnp.zeros_like(g_idx)
    for e in range(N_EXP):
        better = (masked[e] > v2) & (e1 != e)
        e2 = jnp.where(better, e, e2)
        v2 = jnp.where(better, masked[e], v2)
    s1 = jnp.zeros_like(v1)
    s2 = jnp.zeros_like(v1)
    for e in range(N_EXP):
        s1 = jnp.where(e1 == e, score_rows[e], s1)
        s2 = jnp.where(e2 == e, score_rows[e], s2)
    tot = s1 + s2
    w1 = s1 / tot
    w2 = s2 / tot

    e_iota = lax.broadcasted_iota(jnp.int32, (N_EXP, tm), 0)
    hit = (e_iota == e1) | (e_iota == e2)
    onehot = jnp.where(hit, 1.0, 0.0).astype(BF16)
    r_i = lax.broadcasted_iota(jnp.int32, (tm, tm), 0)
    c_i = lax.broadcasted_iota(jnp.int32, (tm, tm), 1)
    upper = jnp.where(r_i <= c_i, 1.0, 0.0).astype(BF16)
    rank = _dot(onehot, upper) - 1.0
    rank1 = jnp.sum(jnp.where(e_iota == e1, rank, 0.0), axis=0, keepdims=True)
    rank2 = jnp.sum(jnp.where(e_iota == e2, rank, 0.0), axis=0, keepdims=True)
    cnt_ref[...] = jnp.broadcast_to(jnp.sum(jnp.where(hit, 1.0, 0.0), axis=1, keepdims=True), cnt_ref.shape)

    route_ref[0:1, :] = e1.astype(F32)
    route_ref[1:2, :] = e2.astype(F32)
    route_ref[2:3, :] = w1
    route_ref[3:4, :] = w2
    route_ref[4:5, :] = rank1
    route_ref[5:6, :] = rank2
    route_ref[6:8, :] = jnp.zeros((2, tm), F32)


def _route(h, g, mod3, layer, rows_per_cond, cond_base, wr_pad, bias_col):
    m = h.shape[0]
    tm = min(TM_ROW, rows_per_cond)

    def row(which):
        return lambda i: (_mod_row(layer, cond_base + (i * tm) // rows_per_cond, which), 0, 0)

    return pl.pallas_call(
        _route_kernel,
        grid=(m // tm,),
        in_specs=[
            pl.BlockSpec((tm, D), lambda i: (i, 0)),
            pl.BlockSpec((1, D), lambda i: (0, 0)),
            pl.BlockSpec((None, 1, D), row(3)),
            pl.BlockSpec((None, 1, D), row(4)),
            pl.BlockSpec((2, D, LANE), lambda i: (0, 0, 0)),
            pl.BlockSpec((N_EXP, 1), lambda i: (0, 0)),
        ],
        out_specs=[
            pl.BlockSpec((tm, D), lambda i: (i, 0)),
            pl.BlockSpec((SUBLANE, tm), lambda i: (0, i)),
            pl.BlockSpec((None, N_EXP, LANE), lambda i: (i, 0, 0)),
        ],
        out_shape=[
            jax.ShapeDtypeStruct((m, D), BF16),
            jax.ShapeDtypeStruct((SUBLANE, m), F32),
            jax.ShapeDtypeStruct((m // tm, N_EXP, LANE), F32),
        ],
        compiler_params=_params(1),
        name="norm_route",
    )(h, g.reshape(1, D), mod3, mod3, wr_pad, bias_col)


SEG_ALIGN = BF16_ROWS
SEG_SIZES = (512, 256, 128, 64, 32, 16)
TAIL_SIZES = (128, 64, 32, 16)
SORT_ROWS = 2 * TM_ROW + N_EXP * SEG_ALIGN


def _seg_copies(src, src_off, dst_hbm, dst_off, length, sem, sizes):
    out = []
    for size in sizes:
        take = (length & size) != 0
        cp = pltpu.make_async_copy(src.at[pl.ds(pl.multiple_of(src_off, SEG_ALIGN), size)],
                                   dst_hbm.at[pl.ds(pl.multiple_of(dst_off, SEG_ALIGN), size)], sem)
        out.append((take, cp))
        src_off = src_off + jnp.where(take, size, 0)
        dst_off = dst_off + jnp.where(take, size, 0)
    return out


def _run_copies(copies):
    for take, cp in copies:
        @pl.when(take)
        def _():
            cp.start()
    for take, cp in copies:
        @pl.when(take)
        def _():
            cp.wait()


def _sort_dispatch_kernel(seg_ref, tail_ref, f_ref, pos_ref, xs_hbm, sbuf, zbuf, sem, *, n_slot_tiles):
    i = pl.program_id(0)
    last = pl.num_programs(0) - 1
    tm = f_ref.shape[0]
    buf = lax.rem(i, 2)
    slot = lax.broadcasted_iota(jnp.int32, (SORT_ROWS, tm), 0)
    onehot = jnp.where((slot == pos_ref[0:1, :]) | (slot == pos_ref[1:2, :]), 1.0, 0.0).astype(BF16)
    for cb in range(D // TN_MM):
        cols = slice(cb * TN_MM, (cb + 1) * TN_MM)
        sbuf[buf, :, cols] = _dot(onehot, f_ref[:, cols]).astype(BF16)

    def run_copies(tile, which):
        copies = []
        for e in range(N_EXP):
            copies += _seg_copies(sbuf.at[which], seg_ref[tile, e], xs_hbm, seg_ref[tile, 2 * N_EXP + e],
                                  seg_ref[tile, N_EXP + e], sem.at[which], SEG_SIZES)
        return copies

    def wait_all(copies):
        for take, cp in copies:
            @pl.when(take)
            def _():
                cp.wait()

    for take, cp in run_copies(i, buf):
        @pl.when(take)
        def _():
            cp.start()

    @pl.when(i > 0)
    def _():
        wait_all(run_copies(jnp.maximum(i - 1, 0), 1 - buf))

    @pl.when(i == last)
    def _():
        wait_all(run_copies(i, buf))
        zsem = sem.at[0]
        zbuf[...] = jnp.zeros_like(zbuf)
        tails = []
        for e in range(N_EXP):
            tails += _seg_copies(zbuf, 0, xs_hbm, tail_ref[e], tail_ref[N_EXP + e], zsem, TAIL_SIZES)
        _run_copies(tails)

        def unused_tile(t):
            return pltpu.make_async_copy(zbuf, xs_hbm.at[pl.ds(pl.multiple_of(t * TM_EXP, TM_EXP), TM_EXP)], zsem)

        def start(t, carry):
            unused_tile(t).start()
            return carry

        def wait(t, carry):
            unused_tile(t).wait()
            return carry

        lax.fori_loop(tail_ref[2 * N_EXP], n_slot_tiles, start, 0)
        lax.fori_loop(tail_ref[2 * N_EXP], n_slot_tiles, wait, 0)


def _sort_dispatch(f, pos, seg, tail, n_slot_tiles):
    t = f.shape[0]
    return pl.pallas_call(
        functools.partial(_sort_dispatch_kernel, n_slot_tiles=n_slot_tiles),
        grid_spec=pltpu.PrefetchScalarGridSpec(
            num_scalar_prefetch=2,
            grid=(t // TM_ROW,),
            in_specs=[
                pl.BlockSpec((TM_ROW, D), lambda i, sg, tl: (i, 0)),
                pl.BlockSpec((SUBLANE, TM_ROW), lambda i, sg, tl: (0, i)),
            ],
            out_specs=pl.BlockSpec(memory_space=pl.ANY),
            scratch_shapes=[pltpu.VMEM((2, SORT_ROWS, D), BF16), pltpu.VMEM((TM_EXP, D), BF16),
                            pltpu.SemaphoreType.DMA((2,))],
        ),
        out_shape=jax.ShapeDtypeStruct((n_slot_tiles * TM_EXP, D), BF16),
        compiler_params=_params(1),
        name="moe_sort_dispatch",
    )(seg, tail, f, pos)


(ST_EXPERT, ST_WCOL, ST_XROW, ST_OROW, ST_OCOL, ST_VALID, ST_FIRST, ST_SLOT, ST_NEXT_E, ST_NEXT_WCOL,
 ST_HAS_NEXT) = range(11)


def _gate_up_weights(w_hbm, wbuf, sem, layer, e, wcol, slot):
    nf = D_FF // TF_EXP
    col_g = pl.multiple_of(wcol * TF_EXP, TF_EXP)
    col_u = pl.multiple_of((nf + wcol) * TF_EXP, TF_EXP)
    return (pltpu.make_async_copy(w_hbm.at[layer, e, :, pl.ds(col_g, TF_EXP)], wbuf.at[slot, 0], sem.at[slot, 0]),
            pltpu.make_async_copy(w_hbm.at[layer, e, :, pl.ds(col_u, TF_EXP)], wbuf.at[slot, 1], sem.at[slot, 1]))


def _gate_up_kernel(st_ref, x_ref, w_hbm, o_ref, wbuf, sem, *, layer):
    s = pl.program_id(0)
    valid = st_ref[ST_VALID, s] > 0
    slot = st_ref[ST_SLOT, s]

    @pl.when(valid & (st_ref[ST_FIRST, s] > 0))
    def _():
        cur = _gate_up_weights(w_hbm, wbuf, sem, layer, st_ref[ST_EXPERT, s], st_ref[ST_WCOL, s], slot)

        @pl.when(s == 0)
        def _():
            for c in cur:
                c.start(priority=WEIGHT_QUEUE)

        for c in cur:
            c.wait()

        @pl.when(st_ref[ST_HAS_NEXT, s] > 0)
        def _():
            for c in _gate_up_weights(w_hbm, wbuf, sem, layer, st_ref[ST_NEXT_E, s], st_ref[ST_NEXT_WCOL, s],
                                      1 - slot):
                c.start(priority=WEIGHT_QUEUE)

    @pl.when(valid)
    def _():
        x = x_ref[...]
        gate = _dot(x, wbuf[slot, 0])
        up = _dot(x, wbuf[slot, 1])
        o_ref[...] = (gate * jax.nn.sigmoid(gate) * up).astype(o_ref.dtype)

    @pl.when(jnp.logical_not(valid))
    def _():
        o_ref[...] = jnp.zeros_like(o_ref)


def _gate_up(xs, w_gate_up, layer, steps):
    n_steps = steps.shape[1]
    return pl.pallas_call(
        functools.partial(_gate_up_kernel, layer=layer),
        grid_spec=pltpu.PrefetchScalarGridSpec(
            num_scalar_prefetch=1,
            grid=(n_steps,),
            in_specs=[
                pl.BlockSpec((TM_EXP, D), lambda s, st: (st[ST_XROW, s], 0)),
                pl.BlockSpec(memory_space=pl.ANY),
            ],
            out_specs=pl.BlockSpec((TM_EXP, TF_EXP), lambda s, st: (st[ST_OROW, s], st[ST_OCOL, s])),
            scratch_shapes=[pltpu.VMEM((2, 2, D, TF_EXP), F32), pltpu.SemaphoreType.DMA((2, 2))],
        ),
        out_shape=jax.ShapeDtypeStruct((xs.shape[0], D_FF), BF16),
        compiler_params=_params(1),
        name="moe_gate_up",
    )(steps, xs, w_gate_up)


TL_EXPERT, TL_SRC, TL_FIRST, TL_SLOT, TL_NEXT_E, TL_HAS_NEXT = range(6)


def _down_weights(w_hbm, wbuf, sem, layer, e, slot):
    return pltpu.make_async_copy(w_hbm.at[layer, e], wbuf.at[slot], sem.at[slot])


def _down_kernel(tl_ref, a_ref, w_hbm, o_ref, wbuf, sem, *, layer):
    i = pl.program_id(0)
    valid = tl_ref[TL_SRC, i] == i
    slot = tl_ref[TL_SLOT, i]

    @pl.when(valid & (tl_ref[TL_FIRST, i] > 0))
    def _():
        cur = _down_weights(w_hbm, wbuf, sem, layer, tl_ref[TL_EXPERT, i], slot)

        @pl.when(i == 0)
        def _():
            cur.start(priority=WEIGHT_QUEUE)

        cur.wait()

        @pl.when(tl_ref[TL_HAS_NEXT, i] > 0)
        def _():
            _down_weights(w_hbm, wbuf, sem, layer, tl_ref[TL_NEXT_E, i], 1 - slot).start(priority=WEIGHT_QUEUE)

    @pl.when(valid)
    def _():
        o_ref[...] = _dot(a_ref[...], wbuf[slot]).astype(o_ref.dtype)

    @pl.when(jnp.logical_not(valid))
    def _():
        o_ref[...] = jnp.zeros_like(o_ref)


def _down(act, w_down, layer, tiles):
    nt = tiles.shape[1]
    return pl.pallas_call(
        functools.partial(_down_kernel, layer=layer),
        grid_spec=pltpu.PrefetchScalarGridSpec(
            num_scalar_prefetch=1,
            grid=(nt,),
            in_specs=[
                pl.BlockSpec((TM_EXP, D_FF), lambda i, tl: (tl[TL_SRC, i], 0)),
                pl.BlockSpec(memory_space=pl.ANY),
            ],
            out_specs=pl.BlockSpec((TM_EXP, D), lambda i, tl: (i, 0)),
            scratch_shapes=[pltpu.VMEM((2, D_FF, D), F32), pltpu.SemaphoreType.DMA((2,))],
        ),
        out_shape=jax.ShapeDtypeStruct((act.shape[0], D), BF16),
        compiler_params=_params(1),
        name="moe_down",
    )(tiles, act, w_down)


def _combine_kernel(seg_ref, y_hbm, pos_ref, route_ref, h_hbm, gate_ref, *rest, tail):
    if tail == "next":
        ng_ref, nsh_ref, nsc_ref, o_ref, a_ref, ybuf, hbuf, sem, hsem = rest
    elif tail == "final":
        ng_ref, o_ref, ybuf, hbuf, sem, hsem = rest
    else:
        o_ref, ybuf, hbuf, sem, hsem = rest
    i = pl.program_id(0)
    tm = o_ref.shape[0]
    h_copy = pltpu.make_async_copy(h_hbm.at[pl.ds(pl.multiple_of(i * tm, tm), tm)], hbuf, hsem)
    h_copy.start()

    def run_copies(tile):
        copies = []
        for e in range(N_EXP):
            copies += _seg_copies(y_hbm, seg_ref[tile, 2 * N_EXP + e], ybuf, seg_ref[tile, e],
                                  seg_ref[tile, N_EXP + e], sem, SEG_SIZES)
        return copies

    def start(copies):
        for take, cp in copies:
            @pl.when(take)
            def _():
                cp.start()

    @pl.when(i == 0)
    def _():
        ybuf[...] = jnp.zeros_like(ybuf)
        start(run_copies(0))

    for take, cp in run_copies(i):
        @pl.when(take)
        def _():
            cp.wait()

    slot = lax.broadcasted_iota(jnp.int32, (SORT_ROWS, tm), 0)
    pick = jnp.where(slot == pos_ref[0:1, :], route_ref[2:3, :],
                     jnp.where(slot == pos_ref[1:2, :], route_ref[3:4, :], 0.0)).astype(BF16)
    for cb in range(D // TN_MM):
        cols = slice(cb * TN_MM, (cb + 1) * TN_MM)
        o_ref[:, cols] = lax.dot_general(pick, ybuf[:, cols], (((0,), (0,)), ((), ())),
                                         preferred_element_type=F32)

    @pl.when(i + 1 < pl.num_programs(0))
    def _():
        start(run_copies(jnp.minimum(i + 1, pl.num_programs(0) - 1)))

    h_copy.wait()
    out = hbuf[...] + gate_ref[...] * o_ref[...]
    if tail == "final":
        out = _rms(out, ng_ref[...])
    o_ref[...] = out
    if tail == "next":
        a_ref[...] = (_rms(out, ng_ref[...]) * (1.0 + nsc_ref[...]) + nsh_ref[...]).astype(a_ref.dtype)


def _combine(y, seg, pos, route, h, mod3, layer, rows_per_cond, cond_base, tail, tail_g):
    m = h.shape[0]
    tm = TM_ROW

    def mod_spec(lyr, which):
        return pl.BlockSpec((None, 1, D),
                            lambda i, sg: (_mod_row(lyr, cond_base + (i * tm) // rows_per_cond, which), 0, 0))

    row_spec = pl.BlockSpec((tm, D), lambda i, sg: (i, 0))
    tok_spec = pl.BlockSpec((SUBLANE, tm), lambda i, sg: (0, i))
    in_specs = [pl.BlockSpec(memory_space=pl.ANY), tok_spec, tok_spec, pl.BlockSpec(memory_space=pl.ANY),
                mod_spec(layer, 5)]
    args = [seg, y, pos, route, h, mod3]
    out_specs, out_shape = row_spec, jax.ShapeDtypeStruct((m, D), F32)
    if tail is not None:
        in_specs.append(pl.BlockSpec((1, D), lambda i, sg: (0, 0)))
        args.append(tail_g.reshape(1, D))
    if tail == "next":
        in_specs += [mod_spec(layer + 1, 0), mod_spec(layer + 1, 1)]
        args += [mod3, mod3]
        out_specs = [row_spec, row_spec]
        out_shape = [out_shape, jax.ShapeDtypeStruct((m, D), BF16)]
    return pl.pallas_call(
        functools.partial(_combine_kernel, tail=tail),
        grid_spec=pltpu.PrefetchScalarGridSpec(
            num_scalar_prefetch=1,
            grid=(m // tm,),
            in_specs=in_specs,
            out_specs=out_specs,
            scratch_shapes=[pltpu.VMEM((SORT_ROWS, D), BF16), pltpu.VMEM((tm, D), F32),
                            pltpu.SemaphoreType.DMA(()), pltpu.SemaphoreType.DMA(())],
        ),
        out_shape=out_shape,
        compiler_params=_params(1),
        name="moe_combine",
    )(*args)


def _moe_block(h, norm_g, mod3, layer, rows_per_cond, cond_base, wr_pad, bias_col, w_gate_up, w_down,
               tail=None, tail_g=None):
    t = h.shape[0]
    f, route, cnt = _route(h, norm_g, mod3, layer, rows_per_cond, cond_base, wr_pad, bias_col)

    i32 = jnp.int32
    e_ids = jnp.arange(N_EXP, dtype=i32)

    def take(table, idx):
        return jnp.sum(jnp.where(idx[:, None] == e_ids[None, :], table[None, :], 0), axis=1).astype(i32)

    def bucket(ends, pos):
        return jnp.sum((ends[None, :] <= pos[:, None]).astype(i32), axis=1)

    n_tok_tiles = t // TM_ROW
    cnt_tile = cnt[:, :, 0].astype(i32)
    seg_len = (cnt_tile + SEG_ALIGN - 1) // SEG_ALIGN * SEG_ALIGN
    rows_e = jnp.sum(seg_len, axis=0)
    tiles_per_e = (rows_e + TM_EXP - 1) // TM_EXP
    tile_end = jnp.cumsum(tiles_per_e).astype(i32)
    tile_start = tile_end - tiles_per_e
    row_start = tile_start * TM_EXP
    seg_out = row_start[None, :] + jnp.cumsum(seg_len, axis=0) - seg_len
    seg_in = jnp.cumsum(seg_len, axis=1) - seg_len
    seg = jnp.concatenate([seg_in, seg_len, seg_out], axis=1).astype(i32)

    def take_tile(table, e):
        e_tiles = e.reshape(n_tok_tiles, TM_ROW)
        hit = e_tiles[:, :, None] == e_ids[None, None, :]
        return jnp.sum(jnp.where(hit, table[:, None, :], 0), axis=2).reshape(t).astype(i32)

    e1 = route[0].astype(i32)
    e2 = route[1].astype(i32)
    r1 = route[4].astype(i32)
    r2 = route[5].astype(i32)
    pos = jnp.zeros((SUBLANE, t), i32).at[0].set(take_tile(seg_in, e1) + r1).at[1].set(take_tile(seg_in, e2) + r2)
    nt = (2 * t + n_tok_tiles * N_EXP * SEG_ALIGN) // TM_EXP + N_EXP
    n_valid = jnp.maximum(tile_end[-1], 1)
    tail_tab = jnp.concatenate([row_start + rows_e, tiles_per_e * TM_EXP - rows_e, tile_end[-1:]]).astype(i32)

    nonempty = tiles_per_e > 0
    order = (jnp.cumsum(nonempty.astype(i32)) - nonempty.astype(i32)).astype(i32)
    later = nonempty[None, :] & (e_ids[None, :] > e_ids[:, None])
    next_e = jnp.min(jnp.where(later, e_ids[None, :], N_EXP), axis=1).astype(i32)

    tile_id = jnp.arange(nt, dtype=i32)
    tile_src = jnp.minimum(tile_id, n_valid - 1)
    tile_e = jnp.minimum(bucket(tile_end, tile_src), N_EXP - 1)
    tile_next = take(next_e, tile_e)
    tiles = jnp.stack([
        tile_e, tile_src,
        (tile_src == take(tile_start, tile_e)).astype(i32),
        take(order, tile_e) % 2,
        jnp.minimum(tile_next, N_EXP - 1),
        (tile_next < N_EXP).astype(i32),
    ]).astype(i32)

    nf = D_FF // TF_EXP
    step_id = jnp.arange(nf * nt, dtype=i32)
    step_valid = step_id < nf * n_valid
    sid = jnp.minimum(step_id, nf * n_valid - 1)
    step_e = jnp.minimum(bucket(nf * tile_end, sid), N_EXP - 1)
    local = sid - nf * take(tile_start, step_e)
    n_e = jnp.maximum(take(tiles_per_e, step_e), 1)
    step_f = local // n_e
    step_r = take(tile_start, step_e) + local % n_e
    pad = step_id - nf * n_valid
    last_col = step_f == nf - 1
    step_next_e = jnp.where(last_col, take(next_e, step_e), step_e)
    steps = jnp.stack([
        step_e, step_f, step_r,
        jnp.where(step_valid, step_r, n_valid + pad // nf),
        jnp.where(step_valid, step_f, pad % nf),
        step_valid.astype(i32),
        (local % n_e == 0).astype(i32),
        (nf * take(order, step_e) + step_f) % 2,
        jnp.minimum(step_next_e, N_EXP - 1),
        jnp.where(last_col, 0, step_f + 1),
        (step_next_e < N_EXP).astype(i32),
    ]).astype(i32)

    xs = _sort_dispatch(f, pos, seg, tail_tab, nt)
    act = _gate_up(xs, w_gate_up, layer, steps)
    y = _down(act, w_down, layer, tiles)

    return _combine(y, seg, pos, route, h, mod3, layer, rows_per_cond, cond_base, tail, tail_g)


def _rope_tables(seq):
    rows = seq // GRID_W
    row = jnp.repeat(jnp.arange(rows), GRID_W).astype(F32)
    col = jnp.tile(jnp.arange(GRID_W), rows).astype(F32)
    n_freq = DH // 4
    inv_freq = ROPE_THETA ** (-jnp.arange(n_freq, dtype=F32) / n_freq)
    ang = jnp.concatenate([row[:, None] * inv_freq, col[:, None] * inv_freq], axis=-1)
    cos, sin = jnp.cos(ang), jnp.sin(ang)
    return jnp.concatenate([cos, cos], axis=-1), jnp.concatenate([-sin, sin], axis=-1)


def kernel(x, c, ctx, c_ctx, w_ada, b_ada, norm1_g, norm2_g, attn_w_qkv, attn_w_o, attn_sink, conv_w_in, conv_w,
           conv_w_out, w_router, router_bias, moe_w_gate_up, moe_w_down, final_g):
    batch, seq, _ = x.shape
    n_ctx = ctx.shape[1]
    depth = w_ada.shape[0]
    assert depth == 2 and x.shape[2] == D
    ctx_row = batch

    c8 = jnp.concatenate([c, c_ctx[None, :], jnp.zeros((SUBLANE - batch - 1, D), F32)], axis=0)
    mod3 = _ada(c8, w_ada, b_ada).reshape(depth * SUBLANE * N_MOD, 1, D)
    wr_f32 = jnp.pad(w_router, ((0, 0), (0, LANE - N_EXP)))
    wr_hi = wr_f32.astype(BF16)
    wr_pad = jnp.stack([wr_hi, (wr_f32 - wr_hi.astype(F32)).astype(BF16)])
    bias_col = router_bias.astype(F32).reshape(N_EXP, 1)
    rope = _rope_tables(seq)

    h_lat = x.reshape(batch * seq, D)
    h_ctx = ctx.reshape(batch * n_ctx, D)

    a_lat = _normmod(h_lat, norm1_g[0], mod3, 0, 0, seq, 0)
    a_ctx = _normmod(h_ctx, norm1_g[0], mod3, 0, 0, batch * n_ctx, ctx_row)
    qkv_lat = _qkv_proj(a_lat, attn_w_qkv[0], rope)
    q_tiles = N_HEADS * DH // TN_MM
    q_ctx = _qkv_proj(a_ctx, attn_w_qkv[0], col_tiles=(0, q_tiles))
    kv_ctx = _qkv_proj(a_ctx, attn_w_qkv[0], col_tiles=(q_tiles, QKV // TN_MM - q_tiles))
    o_lat = _attention(qkv_lat, (qkv_lat, N_HEADS), (kv_ctx, 0), attn_sink[0], batch, band=True)
    o_ctx = _attention(q_ctx, (kv_ctx, 0), (kv_ctx, 0), attn_sink[0], batch, band=False)
    h_lat = _proj_resid(o_lat, attn_w_o[0], h_lat, mod3, 0, 2, seq, 0)
    h_ctx = _proj_resid(o_ctx, attn_w_o[0], h_ctx, mod3, 0, 2, batch * n_ctx, ctx_row)
    moe_args = (wr_pad, bias_col, moe_w_gate_up, moe_w_down)
    h_lat, a_lat = _moe_block(h_lat, norm2_g[0], mod3, 0, seq, 0, *moe_args, "next", norm1_g[1])
    h_ctx, a_ctx = _moe_block(h_ctx, norm2_g[0], mod3, 0, batch * n_ctx, ctx_row, *moe_args, "next", norm1_g[1])

    b_gate, u = _conv_in(a_lat, conv_w_in[0])
    z = _conv_gate(b_gate, u, conv_w[0], seq)
    h_lat = _proj_resid(z, conv_w_out[0], h_lat, mod3, 1, 2, seq, 0)
    h_lat = _moe_block(h_lat, norm2_g[1], mod3, 1, seq, 0, *moe_args, "final", final_g)
    return h_lat.reshape(batch, seq, D)
```

```python
import functools

import jax
import jax.numpy as jnp
from jax import lax
from jax.experimental import pallas as pl
from jax.experimental.pallas import tpu as pltpu

D = 4096
N_HEADS = 32
N_KV = 8
GQA = 4
DH = 128
QKV = (N_HEADS + 2 * N_KV) * DH
GRID_W = 64
WINDOW = 128
ROPE_THETA = 10000.0
N_EXP = 16
N_GRP = 4
EXP_PER_GRP = 4
D_FF = 1024
N_MOD = 6
EPS = 1e-6
NEG = -1e30

LANE = 128
SUBLANE = 8
BF16_ROWS = 2 * SUBLANE
VMEM_LIMIT = 56 * 1024 * 1024
WEIGHT_QUEUE = 1

TM_MM = 512
TN_MM = 1024
TN_CONV = 256
TM_CONV_IN = 1024
TM_ROW = 512
TM_CONV = 256
TM_EXP = 256
TF_EXP = 512

F32 = jnp.float32
BF16 = jnp.bfloat16


def _params(n_axes):
    return pltpu.CompilerParams(dimension_semantics=("arbitrary",) * n_axes,
                                vmem_limit_bytes=VMEM_LIMIT)


def _dot(a, b):
    return lax.dot_general(a, b, (((1,), (0,)), ((), ())), preferred_element_type=F32)


def _dot_nt(a, b):
    return lax.dot_general(a, b, (((1,), (1,)), ((), ())), preferred_element_type=F32)


def _ada_kernel(c_ref, w_ref, b_ref, o_ref):
    s = c_ref[...]
    s = (s * jax.nn.sigmoid(s)).astype(BF16)
    o_ref[...] = _dot(s, w_ref[...]) + b_ref[...]


def _ada(c8, w_ada, b_ada):
    depth = w_ada.shape[0]
    n = w_ada.shape[2]
    return pl.pallas_call(
        _ada_kernel,
        grid=(depth, n // TN_MM),
        in_specs=[
            pl.BlockSpec((SUBLANE, D), lambda l, j: (0, 0)),
            pl.BlockSpec((None, D, TN_MM), lambda l, j: (l, 0, j)),
            pl.BlockSpec((None, 1, TN_MM), lambda l, j: (l, 0, j)),
        ],
        out_specs=pl.BlockSpec((None, SUBLANE, TN_MM), lambda l, j: (l, 0, j)),
        out_shape=jax.ShapeDtypeStruct((depth, SUBLANE, n), F32),
        compiler_params=_params(2),
        name="ada_mod",
    )(c8, w_ada, b_ada.reshape(depth, 1, n))


def _mod_row(layer, cond_row, which):
    return (layer * SUBLANE + cond_row) * N_MOD + which


def _rms(x, g):
    return x * lax.rsqrt(jnp.mean(x * x, axis=-1, keepdims=True) + EPS) * g


def _normmod_kernel(h_ref, g_ref, sh_ref, sc_ref, o_ref):
    y = _rms(h_ref[...], g_ref[...])
    o_ref[...] = (y * (1.0 + sc_ref[...]) + sh_ref[...]).astype(o_ref.dtype)


def _normmod(h, g, mod3, layer, which_shift, rows_per_cond, cond_base):
    m = h.shape[0]
    tm = min(TM_ROW, rows_per_cond)

    def row(which):
        return lambda i: (_mod_row(layer, cond_base + (i * tm) // rows_per_cond, which), 0, 0)

    return pl.pallas_call(
        _normmod_kernel,
        grid=(m // tm,),
        in_specs=[
            pl.BlockSpec((tm, D), lambda i: (i, 0)),
            pl.BlockSpec((1, D), lambda i: (0, 0)),
            pl.BlockSpec((None, 1, D), row(which_shift)),
            pl.BlockSpec((None, 1, D), row(which_shift + 1)),
        ],
        out_specs=pl.BlockSpec((tm, D), lambda i: (i, 0)),
        out_shape=jax.ShapeDtypeStruct((m, D), BF16),
        compiler_params=_params(1),
        name="norm_modulate",
    )(h, g.reshape(1, D), mod3, mod3)


def _mm_plain_kernel(x_ref, w_ref, o_ref):
    o_ref[...] = _dot(x_ref[...], w_ref[...]).astype(o_ref.dtype)


def _mm_rope_kernel(x_ref, w_ref, cos_ref, sin_ref, o_ref, *, n_rope_tiles):
    acc = _dot(x_ref[...], w_ref[...])

    @pl.when(pl.program_id(0) < n_rope_tiles)
    def _():
        cosf = cos_ref[...]
        sinf = sin_ref[...]
        for hh in range(acc.shape[1] // DH):
            xh = acc[:, hh * DH:(hh + 1) * DH]
            o_ref[:, hh * DH:(hh + 1) * DH] = (
                xh * cosf + pltpu.roll(xh, DH // 2, axis=1) * sinf).astype(o_ref.dtype)

    @pl.when(pl.program_id(0) >= n_rope_tiles)
    def _():
        o_ref[...] = acc.astype(o_ref.dtype)


def _mm_resid_kernel(x_ref, w_ref, h_ref, gate_ref, o_ref):
    o_ref[...] = h_ref[...] + gate_ref[...] * _dot(x_ref[...], w_ref[...])


def _qkv_proj(a, w_qkv, rope=None, col_tiles=(0, QKV // TN_MM)):
    m = a.shape[0]
    first, count = col_tiles
    grid = (count, m // TM_MM)
    x_spec = pl.BlockSpec((TM_MM, D), lambda j, i: (i, 0))
    w_spec = pl.BlockSpec((D, TN_MM), lambda j, i: (0, first + j))
    o_spec = pl.BlockSpec((TM_MM, TN_MM), lambda j, i: (i, j))
    out_shape = jax.ShapeDtypeStruct((m, count * TN_MM), BF16)
    if rope is None:
        return pl.pallas_call(_mm_plain_kernel, grid=grid, in_specs=[x_spec, w_spec], out_specs=o_spec,
                              out_shape=out_shape, compiler_params=_params(2), name="qkv_ctx")(a, w_qkv)
    assert col_tiles == (0, QKV // TN_MM)
    cosf, sinf = rope
    seq_tiles = cosf.shape[0] // TM_MM
    t_spec = pl.BlockSpec((TM_MM, DH), lambda j, i: (i % seq_tiles, 0))
    n_rope_tiles = (N_HEADS + N_KV) * DH // TN_MM
    return pl.pallas_call(
        functools.partial(_mm_rope_kernel, n_rope_tiles=n_rope_tiles),
        grid=grid, in_specs=[x_spec, w_spec, t_spec, t_spec], out_specs=o_spec,
        out_shape=out_shape, compiler_params=_params(2), name="qkv_rope")(a, w_qkv, cosf, sinf)


def _proj_resid(x, w, h, mod3, layer, which_gate, rows_per_cond, cond_base):
    m = x.shape[0]
    tm = min(TM_MM, rows_per_cond)

    def gate_map(j, i):
        return (_mod_row(layer, cond_base + (i * tm) // rows_per_cond, which_gate), 0, j)

    return pl.pallas_call(
        _mm_resid_kernel,
        grid=(D // TN_MM, m // tm),
        in_specs=[
            pl.BlockSpec((tm, D), lambda j, i: (i, 0)),
            pl.BlockSpec((D, TN_MM), lambda j, i: (0, j)),
            pl.BlockSpec((tm, TN_MM), lambda j, i: (i, j)),
            pl.BlockSpec((None, 1, TN_MM), gate_map),
        ],
        out_specs=pl.BlockSpec((tm, TN_MM), lambda j, i: (i, j)),
        out_shape=jax.ShapeDtypeStruct((m, D), F32),
        compiler_params=_params(2),
        name="proj_residual",
    )(x, w, h, mod3)


def _attn_kernel(sink_ref, q_ref, k_ref, v_ref, kc_ref, vc_ref, o_ref, *scratch, band, seq):
    kh = pl.program_id(1)
    log2e = 1.4426950408889634
    c = DH ** -0.5 * log2e
    kc = kc_ref[...]
    vc = vc_ref[...]
    cols = GQA * WINDOW
    lane = lax.broadcasted_iota(jnp.int32, (1, cols), 1)
    sink_row = jnp.full((1, cols), sink_ref[kh * GQA + GQA - 1], F32)
    for g in range(GQA - 2, -1, -1):
        sink_row = jnp.where(lane < (g + 1) * WINDOW, sink_ref[kh * GQA + g], sink_row)
    sink_row = sink_row * log2e
    n_band = 3 * WINDOW if band else 0
    if band:
        (mask_ref,) = scratch
        key_minus_query = (lax.broadcasted_iota(jnp.int32, (n_band, cols), 0)
                           - (lax.broadcasted_iota(jnp.int32, (n_band, cols), 1) & (WINDOW - 1)))
        for t in range(3):
            valid = (key_minus_query >= (t - 1) * WINDOW) & (key_minus_query <= (t + 1) * WINDOW)
            mask_ref[t] = jnp.where(valid, 0.0, NEG)

    def block(n, carry):
        q0 = pl.multiple_of(n * WINDOW, WINDOW)
        qb = q_ref[pl.ds(q0, WINDOW), :]
        q4 = jnp.concatenate([qb[:, g * DH:(g + 1) * DH] for g in range(GQA)], axis=0)
        s_ctx = _dot_nt(kc, q4)
        m_raw = jnp.max(s_ctx, axis=0, keepdims=True)
        if band:
            ws = pl.multiple_of(jnp.clip((n - 1) * WINDOW, 0, seq - n_band), WINDOW)
            s_band = _dot_nt(k_ref[pl.ds(ws, n_band), :], q4) + mask_ref[(q0 - ws) // WINDOW]
            m_raw = jnp.maximum(m_raw, jnp.max(s_band, axis=0, keepdims=True))
        m = jnp.maximum(m_raw * c, sink_row)
        p_ctx = jnp.exp2(s_ctx * c - m)
        denom = jnp.sum(p_ctx, axis=0, keepdims=True) + jnp.exp2(sink_row - m)
        o_t = lax.dot_general(vc, p_ctx.astype(BF16), (((0,), (0,)), ((), ())), preferred_element_type=F32)
        if band:
            p_band = jnp.exp2(s_band * c - m)
            denom = denom + jnp.sum(p_band, axis=0, keepdims=True)
            o_t = o_t + lax.dot_general(v_ref[pl.ds(ws, n_band), :], p_band.astype(BF16),
                                        (((0,), (0,)), ((), ())), preferred_element_type=F32)
        o_t = o_t / denom
        for g in range(GQA):
            o_ref[pl.ds(q0, WINDOW), g * DH:(g + 1) * DH] = (
                o_t[:, g * WINDOW:(g + 1) * WINDOW].T.astype(o_ref.dtype))
        return carry

    lax.fori_loop(0, q_ref.shape[0] // WINDOW, block, 0, unroll=8)


def _attention(q_arr, kv, ctx_kv, sink, batch, band):
    qkv_q = q_arr
    qkv_kv, k_col = kv
    qkv_ctx, ck_col = ctx_kv
    seq_q = qkv_q.shape[0] // batch
    seq_k = qkv_kv.shape[0] // batch
    n_ctx = qkv_ctx.shape[0] // batch
    v_col = k_col + N_KV
    cv_col = ck_col + N_KV
    return pl.pallas_call(
        functools.partial(_attn_kernel, band=band, seq=seq_k),
        grid_spec=pltpu.PrefetchScalarGridSpec(
            num_scalar_prefetch=1,
            grid=(batch, N_KV),
            in_specs=[
                pl.BlockSpec((seq_q, GQA * DH), lambda b, k, s: (b, k)),
                pl.BlockSpec((seq_k, DH), lambda b, k, s: (b, k_col + k)),
                pl.BlockSpec((seq_k, DH), lambda b, k, s: (b, v_col + k)),
                pl.BlockSpec((n_ctx, DH), lambda b, k, s: (b, ck_col + k)),
                pl.BlockSpec((n_ctx, DH), lambda b, k, s: (b, cv_col + k)),
            ],
            out_specs=pl.BlockSpec((seq_q, GQA * DH), lambda b, k, s: (b, k)),
            scratch_shapes=[pltpu.VMEM((3, 3 * WINDOW, GQA * WINDOW), F32)] if band else [],
        ),
        out_shape=jax.ShapeDtypeStruct((qkv_q.shape[0], D), BF16),
        compiler_params=_params(2),
        name="window_attn" if band else "ctx_attn",
    )(sink, qkv_q, qkv_kv, qkv_kv, qkv_ctx, qkv_ctx)


def _conv_in_kernel(x_ref, wb_ref, wc_ref, wx_ref, b_ref, u_ref):
    x = x_ref[...]
    b_ref[...] = _dot(x, wb_ref[...]).astype(b_ref.dtype)
    u_ref[...] = (_dot(x, wc_ref[...]) * _dot(x, wx_ref[...])).astype(u_ref.dtype)


def _conv_in(a, w_in):
    m = a.shape[0]
    nt = D // TN_CONV
    o_spec = pl.BlockSpec((TM_CONV_IN, TN_CONV), lambda j, i: (i, j))
    return pl.pallas_call(
        _conv_in_kernel,
        grid=(nt, m // TM_CONV_IN),
        in_specs=[
            pl.BlockSpec((TM_CONV_IN, D), lambda j, i: (i, 0)),
            pl.BlockSpec((D, TN_CONV), lambda j, i: (0, j)),
            pl.BlockSpec((D, TN_CONV), lambda j, i: (0, nt + j)),
            pl.BlockSpec((D, TN_CONV), lambda j, i: (0, 2 * nt + j)),
        ],
        out_specs=[o_spec, o_spec],
        out_shape=[jax.ShapeDtypeStruct((m, D), BF16)] * 2,
        compiler_params=_params(2),
        name="conv_in_proj",
    )(a, w_in, w_in, w_in)


def _conv_gate_kernel(b_ref, u_ref, up_ref, un_ref, w_ref, o_ref, *, seq):
    i = pl.program_id(0)
    tm = u_ref.shape[0]
    u = u_ref[...].astype(F32)
    t0 = i * tm
    prev_row = jnp.where(lax.rem(t0, seq) == 0, 0.0, up_ref[...].astype(F32)[BF16_ROWS - 1:BF16_ROWS, :])
    next_row = jnp.where(lax.rem(t0 + tm, seq) == 0, 0.0, un_ref[...].astype(F32)[0:1, :])
    row = lax.broadcasted_iota(jnp.int32, u.shape, 0)
    u_prev = jnp.where(row == 0, prev_row, pltpu.roll(u, 1, axis=0))
    u_next = jnp.where(row == tm - 1, next_row, pltpu.roll(u, tm - 1, axis=0))
    y = u_prev * w_ref[0:1, :] + u * w_ref[1:2, :] + u_next * w_ref[2:3, :]
    o_ref[...] = (b_ref[...].astype(F32) * y).astype(o_ref.dtype)


def _conv_gate(b, u, conv_w, seq):
    m = u.shape[0]
    tm = TM_CONV
    per = tm // BF16_ROWS
    last = m // BF16_ROWS - 1
    return pl.pallas_call(
        functools.partial(_conv_gate_kernel, seq=seq),
        grid=(m // tm,),
        in_specs=[
            pl.BlockSpec((tm, D), lambda i: (i, 0)),
            pl.BlockSpec((tm, D), lambda i: (i, 0)),
            pl.BlockSpec((BF16_ROWS, D), lambda i: (jnp.maximum(i * per - 1, 0), 0)),
            pl.BlockSpec((BF16_ROWS, D), lambda i: (jnp.minimum((i + 1) * per, last), 0)),
            pl.BlockSpec((3, D), lambda i: (0, 0)),
        ],
        out_specs=pl.BlockSpec((tm, D), lambda i: (i, 0)),
        out_shape=jax.ShapeDtypeStruct((m, D), BF16),
        compiler_params=_params(1),
        name="conv_gate",
    )(b, u, u, u, conv_w)


def _route_kernel(h_ref, g_ref, sh_ref, sc_ref, wr_ref, bias_ref, f_ref, route_ref, cnt_ref):
    tm = h_ref.shape[0]
    f = _rms(h_ref[...], g_ref[...]) * (1.0 + sc_ref[...]) + sh_ref[...]
    f_ref[...] = f.astype(f_ref.dtype)
    f_hi = f.astype(BF16)
    f_lo = (f - f_hi.astype(F32)).astype(BF16)
    logits = _dot(f_hi, wr_ref[0]) + (_dot(f_lo, wr_ref[0]) + _dot(f_hi, wr_ref[1]))
    lt = logits.T[:N_EXP]
    score = jax.nn.sigmoid(lt)
    sel = score + bias_ref[...]
    sel_rows = [sel[e:e + 1] for e in range(N_EXP)]
    score_rows = [score[e:e + 1] for e in range(N_EXP)]

    grp_scores = []
    for gi in range(N_GRP):
        a, b, c, d = sel_rows[gi * EXP_PER_GRP:(gi + 1) * EXP_PER_GRP]
        hi1, lo1 = jnp.maximum(a, b), jnp.minimum(a, b)
        hi2, lo2 = jnp.maximum(c, d), jnp.minimum(c, d)
        grp_scores.append(jnp.maximum(hi1, hi2) + jnp.maximum(jnp.minimum(hi1, hi2), jnp.maximum(lo1, lo2)))
    best = grp_scores[0]
    g_idx = jnp.zeros_like(best, dtype=jnp.int32)
    for gi in range(1, N_GRP):
        better = grp_scores[gi] > best
        g_idx = jnp.where(better, gi, g_idx)
        best = jnp.where(better, grp_scores[gi], best)

    masked = [jnp.where(g_idx == e // EXP_PER_GRP, sel_rows[e], NEG) for e in range(N_EXP)]
    v1 = masked[0]
    e1 = jnp.zeros_like(g_idx)
    for e in range(1, N_EXP):
        better = masked[e] > v1
        e1 = jnp.where(better, e, e1)
        v1 = jnp.where(better, masked[e], v1)
    v2 = jnp.full_like(v1, -jnp.inf)
    e2 = jnp.zeros_like(g_idx)
    for e in range(N_EXP):
        better = (masked[e] > v2) & (e1 != e)
        e2 = jnp.where(better, e, e2)
        v2 = jnp.where(better, masked[e], v2)
    s1 = jnp.zeros_like(v1)
    s2 = jnp.zeros_like(v1)
    for e in range(N_EXP):
        s1 = jnp.where(e1 == e, score_rows[e], s1)
        s2 = jnp.where(e2 == e, score_rows[e], s2)
    tot = s1 + s2
    w1 = s1 / tot
    w2 = s2 / tot

    e_iota = lax.broadcasted_iota(jnp.int32, (N_EXP, tm), 0)
    hit = (e_iota == e1) | (e_iota == e2)
    onehot = jnp.where(hit, 1.0, 0.0).astype(BF16)
    r_i = lax.broadcasted_iota(jnp.int32, (tm, tm), 0)
    c_i = lax.broadcasted_iota(jnp.int32, (tm, tm), 1)
    upper = jnp.where(r_i <= c_i, 1.0, 0.0).astype(BF16)
    rank = _dot(onehot, upper) - 1.0
    rank1 = jnp.sum(jnp.where(e_iota == e1, rank, 0.0), axis=0, keepdims=True)
    rank2 = jnp.sum(jnp.where(e_iota == e2, rank, 0.0), axis=0, keepdims=True)
    cnt_ref[...] = jnp.broadcast_to(jnp.sum(jnp.where(hit, 1.0, 0.0), axis=1, keepdims=True), cnt_ref.shape)

    route_ref[0:1, :] = e1.astype(F32)
    route_ref[1:2, :] = e2.astype(F32)
    route_ref[2:3, :] = w1
    route_ref[3:4, :] = w2
    route_ref[4:5, :] = rank1
    route_ref[5:6, :] = rank2
    route_ref[6:8, :] = jnp.zeros((2, tm), F32)


def _route(h, g, mod3, layer, rows_per_cond, cond_base, wr_pad, bias_col):
    m = h.shape[0]
    tm = min(TM_ROW, rows_per_cond)

    def row(which):
        return lambda i: (_mod_row(layer, cond_base + (i * tm) // rows_per_cond, which), 0, 0)

    return pl.pallas_call(
        _route_kernel,
        grid=(m // tm,),
        in_specs=[
            pl.BlockSpec((tm, D), lambda i: (i, 0)),
            pl.BlockSpec((1, D), lambda i: (0, 0)),
            pl.BlockSpec((None, 1, D), row(3)),
            pl.BlockSpec((None, 1, D), row(4)),
            pl.BlockSpec((2, D, LANE), lambda i: (0, 0, 0)),
            pl.BlockSpec((N_EXP, 1), lambda i: (0, 0)),
        ],
        out_specs=[
            pl.BlockSpec((tm, D), lambda i: (i, 0)),
            pl.BlockSpec((SUBLANE, tm), lambda i: (0, i)),
            pl.BlockSpec((None, N_EXP, LANE), lambda i: (i, 0, 0)),
        ],
        out_shape=[
            jax.ShapeDtypeStruct((m, D), BF16),
            jax.ShapeDtypeStruct((SUBLANE, m), F32),
            jax.ShapeDtypeStruct((m // tm, N_EXP, LANE), F32),
        ],
        compiler_params=_params(1),
        name="norm_route",
    )(h, g.reshape(1, D), mod3, mod3, wr_pad, bias_col)


SEG_ALIGN = BF16_ROWS
SEG_SIZES = (512, 256, 128, 64, 32, 16)
TAIL_SIZES = (128, 64, 32, 16)
SORT_ROWS = 2 * TM_ROW + N_EXP * SEG_ALIGN


def _seg_copies(src, src_off, dst_hbm, dst_off, length, sem, sizes):
    out = []
    for size in sizes:
        take = (length & size) != 0
        cp = pltpu.make_async_copy(src.at[pl.ds(pl.multiple_of(src_off, SEG_ALIGN), size)],
                                   dst_hbm.at[pl.ds(pl.multiple_of(dst_off, SEG_ALIGN), size)], sem)
        out.append((take, cp))
        src_off = src_off + jnp.where(take, size, 0)
        dst_off = dst_off + jnp.where(take, size, 0)
    return out


def _run_copies(copies):
    for take, cp in copies:
        @pl.when(take)
        def _():
            cp.start()
    for take, cp in copies:
        @pl.when(take)
        def _():
            cp.wait()


def _sort_dispatch_kernel(seg_ref, tail_ref, f_ref, pos_ref, xs_hbm, sbuf, zbuf, sem, *, n_slot_tiles):
    i = pl.program_id(0)
    last = pl.num_programs(0) - 1
    tm = f_ref.shape[0]
    buf = lax.rem(i, 2)
    slot = lax.broadcasted_iota(jnp.int32, (SORT_ROWS, tm), 0)
    onehot = jnp.where((slot == pos_ref[0:1, :]) | (slot == pos_ref[1:2, :]), 1.0, 0.0).astype(BF16)
    for cb in range(D // TN_MM):
        cols = slice(cb * TN_MM, (cb + 1) * TN_MM)
        sbuf[buf, :, cols] = _dot(onehot, f_ref[:, cols]).astype(BF16)

    def run_copies(tile, which):
        copies = []
        for e in range(N_EXP):
            copies += _seg_copies(sbuf.at[which], seg_ref[tile, e], xs_hbm, seg_ref[tile, 2 * N_EXP + e],
                                  seg_ref[tile, N_EXP + e], sem.at[which], SEG_SIZES)
        return copies

    def wait_all(copies):
        for take, cp in copies:
            @pl.when(take)
            def _():
                cp.wait()

    for take, cp in run_copies(i, buf):
        @pl.when(take)
        def _():
            cp.start()

    @pl.when(i > 0)
    def _():
        wait_all(run_copies(jnp.maximum(i - 1, 0), 1 - buf))

    @pl.when(i == last)
    def _():
        wait_all(run_copies(i, buf))
        zsem = sem.at[0]
        zbuf[...] = jnp.zeros_like(zbuf)
        tails = []
        for e in range(N_EXP):
            tails += _seg_copies(zbuf, 0, xs_hbm, tail_ref[e], tail_ref[N_EXP + e], zsem, TAIL_SIZES)
        _run_copies(tails)

        def unused_tile(t):
            return pltpu.make_async_copy(zbuf, xs_hbm.at[pl.ds(pl.multiple_of(t * TM_EXP, TM_EXP), TM_EXP)], zsem)

        def start(t, carry):
            unused_tile(t).start()
            return carry

        def wait(t, carry):
            unused_tile(t).wait()
            return carry

        lax.fori_loop(tail_ref[2 * N_EXP], n_slot_tiles, start, 0)
        lax.fori_loop(tail_ref[2 * N_EXP], n_slot_tiles, wait, 0)


def _sort_dispatch(f, pos, seg, tail, n_slot_tiles):
    t = f.shape[0]
    return pl.pallas_call(
        functools.partial(_sort_dispatch_kernel, n_slot_tiles=n_slot_tiles),
        grid_spec=pltpu.PrefetchScalarGridSpec(
            num_scalar_prefetch=2,
            grid=(t // TM_ROW,),
            in_specs=[
                pl.BlockSpec((TM_ROW, D), lambda i, sg, tl: (i, 0)),
                pl.BlockSpec((SUBLANE, TM_ROW), lambda i, sg, tl: (0, i)),
            ],
            out_specs=pl.BlockSpec(memory_space=pl.ANY),
            scratch_shapes=[pltpu.VMEM((2, SORT_ROWS, D), BF16), pltpu.VMEM((TM_EXP, D), BF16),
                            pltpu.SemaphoreType.DMA((2,))],
        ),
        out_shape=jax.ShapeDtypeStruct((n_slot_tiles * TM_EXP, D), BF16),
        compiler_params=_params(1),
        name="moe_sort_dispatch",
    )(seg, tail, f, pos)


(ST_EXPERT, ST_WCOL, ST_XROW, ST_OROW, ST_OCOL, ST_VALID, ST_FIRST, ST_SLOT, ST_NEXT_E, ST_NEXT_WCOL,
 ST_HAS_NEXT) = range(11)


def _gate_up_weights(w_hbm, wbuf, sem, layer, e, wcol, slot):
    nf = D_FF // TF_EXP
    col_g = pl.multiple_of(wcol * TF_EXP, TF_EXP)
    col_u = pl.multiple_of((nf + wcol) * TF_EXP, TF_EXP)
    return (pltpu.make_async_copy(w_hbm.at[layer, e, :, pl.ds(col_g, TF_EXP)], wbuf.at[slot, 0], sem.at[slot, 0]),
            pltpu.make_async_copy(w_hbm.at[layer, e, :, pl.ds(col_u, TF_EXP)], wbuf.at[slot, 1], sem.at[slot, 1]))


def _gate_up_kernel(st_ref, x_ref, w_hbm, o_ref, wbuf, sem, *, layer):
    s = pl.program_id(0)
    valid = st_ref[ST_VALID, s] > 0
    slot = st_ref[ST_SLOT, s]

    @pl.when(valid & (st_ref[ST_FIRST, s] > 0))
    def _():
        cur = _gate_up_weights(w_hbm, wbuf, sem, layer, st_ref[ST_EXPERT, s], st_ref[ST_WCOL, s], slot)

        @pl.when(s == 0)
        def _():
            for c in cur:
                c.start(priority=WEIGHT_QUEUE)

        for c in cur:
            c.wait()

        @pl.when(st_ref[ST_HAS_NEXT, s] > 0)
        def _():
            for c in _gate_up_weights(w_hbm, wbuf, sem, layer, st_ref[ST_NEXT_E, s], st_ref[ST_NEXT_WCOL, s],
                                      1 - slot):
                c.start(priority=WEIGHT_QUEUE)

    @pl.when(valid)
    def _():
        x = x_ref[...]
        gate = _dot(x, wbuf[slot, 0])
        up = _dot(x, wbuf[slot, 1])
        o_ref[...] = (gate * jax.nn.sigmoid(gate) * up).astype(o_ref.dtype)

    @pl.when(jnp.logical_not(valid))
    def _():
        o_ref[...] = jnp.zeros_like(o_ref)


def _gate_up(xs, w_gate_up, layer, steps):
    n_steps = steps.shape[1]
    return pl.pallas_call(
        functools.partial(_gate_up_kernel, layer=layer),
        grid_spec=pltpu.PrefetchScalarGridSpec(
            num_scalar_prefetch=1,
            grid=(n_steps,),
            in_specs=[
                pl.BlockSpec((TM_EXP, D), lambda s, st: (st[ST_XROW, s], 0)),
                pl.BlockSpec(memory_space=pl.ANY),
            ],
            out_specs=pl.BlockSpec((TM_EXP, TF_EXP), lambda s, st: (st[ST_OROW, s], st[ST_OCOL, s])),
            scratch_shapes=[pltpu.VMEM((2, 2, D, TF_EXP), F32), pltpu.SemaphoreType.DMA((2, 2))],
        ),
        out_shape=jax.ShapeDtypeStruct((xs.shape[0], D_FF), BF16),
        compiler_params=_params(1),
        name="moe_gate_up",
    )(steps, xs, w_gate_up)


TL_EXPERT, TL_SRC, TL_FIRST, TL_SLOT, TL_NEXT_E, TL_HAS_NEXT = range(6)


def _down_weights(w_hbm, wbuf, sem, layer, e, slot):
    return pltpu.make_async_copy(w_hbm.at[layer, e], wbuf.at[slot], sem.at[slot])


def _down_kernel(tl_ref, a_ref, w_hbm, o_ref, wbuf, sem, *, layer):
    i = pl.program_id(0)
    valid = tl_ref[TL_SRC, i] == i
    slot = tl_ref[TL_SLOT, i]

    @pl.when(valid & (tl_ref[TL_FIRST, i] > 0))
    def _():
        cur = _down_weights(w_hbm, wbuf, sem, layer, tl_ref[TL_EXPERT, i], slot)

        @pl.when(i == 0)
        def _():
            cur.start(priority=WEIGHT_QUEUE)

        cur.wait()

        @pl.when(tl_ref[TL_HAS_NEXT, i] > 0)
        def _():
            _down_weights(w_hbm, wbuf, sem, layer, tl_ref[TL_NEXT_E, i], 1 - slot).start(priority=WEIGHT_QUEUE)

    @pl.when(valid)
    def _():
        o_ref[...] = _dot(a_ref[...], wbuf[slot]).astype(o_ref.dtype)

    @pl.when(jnp.logical_not(valid))
    def _():
        o_ref[...] = jnp.zeros_like(o_ref)


def _down(act, w_down, layer, tiles):
    nt = tiles.shape[1]
    return pl.pallas_call(
        functools.partial(_down_kernel, layer=layer),
        grid_spec=pltpu.PrefetchScalarGridSpec(
            num_scalar_prefetch=1,
            grid=(nt,),
            in_specs=[
                pl.BlockSpec((TM_EXP, D_FF), lambda i, tl: (tl[TL_SRC, i], 0)),
                pl.BlockSpec(memory_space=pl.ANY),
            ],
            out_specs=pl.BlockSpec((TM_EXP, D), lambda i, tl: (i, 0)),
            scratch_shapes=[pltpu.VMEM((2, D_FF, D), F32), pltpu.SemaphoreType.DMA((2,))],
        ),
        out_shape=jax.ShapeDtypeStruct((act.shape[0], D), BF16),
        compiler_params=_params(1),
        name="moe_down",
    )(tiles, act, w_down)


def _combine_kernel(seg_ref, y_hbm, pos_ref, route_ref, h_hbm, gate_ref, *rest, tail):
    if tail == "next":
        ng_ref, nsh_ref, nsc_ref, o_ref, a_ref, ybuf, hbuf, sem, hsem = rest
    elif tail == "final":
        ng_ref, o_ref, ybuf, hbuf, sem, hsem = rest
    else:
        o_ref, ybuf, hbuf, sem, hsem = rest
    i = pl.program_id(0)
    tm = o_ref.shape[0]
    h_copy = pltpu.make_async_copy(h_hbm.at[pl.ds(pl.multiple_of(i * tm, tm), tm)], hbuf, hsem)
    h_copy.start()

    def run_copies(tile):
        copies = []
        for e in range(N_EXP):
            copies += _seg_copies(y_hbm, seg_ref[tile, 2 * N_EXP + e], ybuf, seg_ref[tile, e],
                                  seg_ref[tile, N_EXP + e], sem, SEG_SIZES)
        return copies

    def start(copies):
        for take, cp in copies:
            @pl.when(take)
            def _():
                cp.start()

    @pl.when(i == 0)
    def _():
        ybuf[...] = jnp.zeros_like(ybuf)
        start(run_copies(0))

    for take, cp in run_copies(i):
        @pl.when(take)
        def _():
            cp.wait()

    slot = lax.broadcasted_iota(jnp.int32, (SORT_ROWS, tm), 0)
    pick = jnp.where(slot == pos_ref[0:1, :], route_ref[2:3, :],
                     jnp.where(slot == pos_ref[1:2, :], route_ref[3:4, :], 0.0)).astype(BF16)
    def picked(cb):
        cols = slice(cb * TN_MM, (cb + 1) * TN_MM)
        return lax.dot_general(pick, ybuf[:, cols], (((0,), (0,)), ((), ())), preferred_element_type=F32)

    first = picked(0)
    h_copy.wait()
    ssq = jnp.zeros((tm, 1), F32)
    for cb in range(D // TN_MM):
        cols = slice(cb * TN_MM, (cb + 1) * TN_MM)
        out = hbuf[:, cols] + gate_ref[:, cols] * (first if cb == 0 else picked(cb))
        o_ref[:, cols] = out
        ssq = ssq + jnp.sum(out * out, axis=1, keepdims=True)

    @pl.when(i + 1 < pl.num_programs(0))
    def _():
        start(run_copies(jnp.minimum(i + 1, pl.num_programs(0) - 1)))

    if tail is not None:
        normed = o_ref[...] * lax.rsqrt(ssq * (1.0 / D) + EPS) * ng_ref[...]
        if tail == "final":
            o_ref[...] = normed
        else:
            a_ref[...] = (normed * (1.0 + nsc_ref[...]) + nsh_ref[...]).astype(a_ref.dtype)


def _combine(y, seg, pos, route, h, mod3, layer, rows_per_cond, cond_base, tail, tail_g):
    m = h.shape[0]
    tm = TM_ROW

    def mod_spec(lyr, which):
        return pl.BlockSpec((None, 1, D),
                            lambda i, sg: (_mod_row(lyr, cond_base + (i * tm) // rows_per_cond, which), 0, 0))

    row_spec = pl.BlockSpec((tm, D), lambda i, sg: (i, 0))
    tok_spec = pl.BlockSpec((SUBLANE, tm), lambda i, sg: (0, i))
    in_specs = [pl.BlockSpec(memory_space=pl.ANY), tok_spec, tok_spec, pl.BlockSpec(memory_space=pl.ANY),
                mod_spec(layer, 5)]
    args = [seg, y, pos, route, h, mod3]
    out_specs, out_shape = row_spec, jax.ShapeDtypeStruct((m, D), F32)
    if tail is not None:
        in_specs.append(pl.BlockSpec((1, D), lambda i, sg: (0, 0)))
        args.append(tail_g.reshape(1, D))
    if tail == "next":
        in_specs += [mod_spec(layer + 1, 0), mod_spec(layer + 1, 1)]
        args += [mod3, mod3]
        out_specs = [row_spec, row_spec]
        out_shape = [out_shape, jax.ShapeDtypeStruct((m, D), BF16)]
    return pl.pallas_call(
        functools.partial(_combine_kernel, tail=tail),
        grid_spec=pltpu.PrefetchScalarGridSpec(
            num_scalar_prefetch=1,
            grid=(m // tm,),
            in_specs=in_specs,
            out_specs=out_specs,
            scratch_shapes=[pltpu.VMEM((SORT_ROWS, D), BF16), pltpu.VMEM((tm, D), F32),
                            pltpu.SemaphoreType.DMA(()), pltpu.SemaphoreType.DMA(())],
        ),
        out_shape=out_shape,
        compiler_params=_params(1),
        name="moe_combine",
    )(*args)


def _moe_block(h, norm_g, mod3, layer, rows_per_cond, cond_base, wr_pad, bias_col, w_gate_up, w_down,
               tail=None, tail_g=None):
    t = h.shape[0]
    f, route, cnt = _route(h, norm_g, mod3, layer, rows_per_cond, cond_base, wr_pad, bias_col)

    i32 = jnp.int32
    e_ids = jnp.arange(N_EXP, dtype=i32)

    def take(table, idx):
        return jnp.sum(jnp.where(idx[:, None] == e_ids[None, :], table[None, :], 0), axis=1).astype(i32)

    def bucket(ends, pos):
        return jnp.sum((ends[None, :] <= pos[:, None]).astype(i32), axis=1)

    n_tok_tiles = t // TM_ROW
    cnt_tile = cnt[:, :, 0].astype(i32)
    seg_len = (cnt_tile + SEG_ALIGN - 1) // SEG_ALIGN * SEG_ALIGN
    rows_e = jnp.sum(seg_len, axis=0)
    tiles_per_e = (rows_e + TM_EXP - 1) // TM_EXP
    tile_end = jnp.cumsum(tiles_per_e).astype(i32)
    tile_start = tile_end - tiles_per_e
    row_start = tile_start * TM_EXP
    seg_out = row_start[None, :] + jnp.cumsum(seg_len, axis=0) - seg_len
    seg_in = jnp.cumsum(seg_len, axis=1) - seg_len
    seg = jnp.concatenate([seg_in, seg_len, seg_out], axis=1).astype(i32)

    def take_tile(table, e):
        e_tiles = e.reshape(n_tok_tiles, TM_ROW)
        hit = e_tiles[:, :, None] == e_ids[None, None, :]
        return jnp.sum(jnp.where(hit, table[:, None, :], 0), axis=2).reshape(t).astype(i32)

    e1 = route[0].astype(i32)
    e2 = route[1].astype(i32)
    r1 = route[4].astype(i32)
    r2 = route[5].astype(i32)
    pos = jnp.zeros((SUBLANE, t), i32).at[0].set(take_tile(seg_in, e1) + r1).at[1].set(take_tile(seg_in, e2) + r2)
    nt = (2 * t + n_tok_tiles * N_EXP * SEG_ALIGN) // TM_EXP + N_EXP
    n_valid = jnp.maximum(tile_end[-1], 1)
    tail_tab = jnp.concatenate([row_start + rows_e, tiles_per_e * TM_EXP - rows_e, tile_end[-1:]]).astype(i32)

    nonempty = tiles_per_e > 0
    order = (jnp.cumsum(nonempty.astype(i32)) - nonempty.astype(i32)).astype(i32)
    later = nonempty[None, :] & (e_ids[None, :] > e_ids[:, None])
    next_e = jnp.min(jnp.where(later, e_ids[None, :], N_EXP), axis=1).astype(i32)

    tile_id = jnp.arange(nt, dtype=i32)
    tile_src = jnp.minimum(tile_id, n_valid - 1)
    tile_e = jnp.minimum(bucket(tile_end, tile_src), N_EXP - 1)
    tile_next = take(next_e, tile_e)
    tiles = jnp.stack([
        tile_e, tile_src,
        (tile_src == take(tile_start, tile_e)).astype(i32),
        take(order, tile_e) % 2,
        jnp.minimum(tile_next, N_EXP - 1),
        (tile_next < N_EXP).astype(i32),
    ]).astype(i32)

    nf = D_FF // TF_EXP
    step_id = jnp.arange(nf * nt, dtype=i32)
    step_valid = step_id < nf * n_valid
    sid = jnp.minimum(step_id, nf * n_valid - 1)
    step_e = jnp.minimum(bucket(nf * tile_end, sid), N_EXP - 1)
    local = sid - nf * take(tile_start, step_e)
    n_e = jnp.maximum(take(tiles_per_e, step_e), 1)
    step_f = local // n_e
    step_r = take(tile_start, step_e) + local % n_e
    pad = step_id - nf * n_valid
    last_col = step_f == nf - 1
    step_next_e = jnp.where(last_col, take(next_e, step_e), step_e)
    steps = jnp.stack([
        step_e, step_f, step_r,
        jnp.where(step_valid, step_r, n_valid + pad // nf),
        jnp.where(step_valid, step_f, pad % nf),
        step_valid.astype(i32),
        (local % n_e == 0).astype(i32),
        (nf * take(order, step_e) + step_f) % 2,
        jnp.minimum(step_next_e, N_EXP - 1),
        jnp.where(last_col, 0, step_f + 1),
        (step_next_e < N_EXP).astype(i32),
    ]).astype(i32)

    xs = _sort_dispatch(f, pos, seg, tail_tab, nt)
    act = _gate_up(xs, w_gate_up, layer, steps)
    y = _down(act, w_down, layer, tiles)

    return _combine(y, seg, pos, route, h, mod3, layer, rows_per_cond, cond_base, tail, tail_g)


def _rope_tables(seq):
    rows = seq // GRID_W
    row = jnp.repeat(jnp.arange(rows), GRID_W).astype(F32)
    col = jnp.tile(jnp.arange(GRID_W), rows).astype(F32)
    n_freq = DH // 4
    inv_freq = ROPE_THETA ** (-jnp.arange(n_freq, dtype=F32) / n_freq)
    ang = jnp.concatenate([row[:, None] * inv_freq, col[:, None] * inv_freq], axis=-1)
    cos, sin = jnp.cos(ang), jnp.sin(ang)
    return jnp.concatenate([cos, cos], axis=-1), jnp.concatenate([-sin, sin], axis=-1)


def kernel(x, c, ctx, c_ctx, w_ada, b_ada, norm1_g, norm2_g, attn_w_qkv, attn_w_o, attn_sink, conv_w_in, conv_w,
           conv_w_out, w_router, router_bias, moe_w_gate_up, moe_w_down, final_g):
    batch, seq, _ = x.shape
    n_ctx = ctx.shape[1]
    depth = w_ada.shape[0]
    assert depth == 2 and x.shape[2] == D
    ctx_row = batch

    c8 = jnp.concatenate([c, c_ctx[None, :], jnp.zeros((SUBLANE - batch - 1, D), F32)], axis=0)
    mod3 = _ada(c8, w_ada, b_ada).reshape(depth * SUBLANE * N_MOD, 1, D)
    wr_f32 = jnp.pad(w_router, ((0, 0), (0, LANE - N_EXP)))
    wr_hi = wr_f32.astype(BF16)
    wr_pad = jnp.stack([wr_hi, (wr_f32 - wr_hi.astype(F32)).astype(BF16)])
    bias_col = router_bias.astype(F32).reshape(N_EXP, 1)
    rope = _rope_tables(seq)

    h_lat = x.reshape(batch * seq, D)
    h_ctx = ctx.reshape(batch * n_ctx, D)

    a_lat = _normmod(h_lat, norm1_g[0], mod3, 0, 0, seq, 0)
    a_ctx = _normmod(h_ctx, norm1_g[0], mod3, 0, 0, batch * n_ctx, ctx_row)
    qkv_lat = _qkv_proj(a_lat, attn_w_qkv[0], rope)
    q_tiles = N_HEADS * DH // TN_MM
    q_ctx = _qkv_proj(a_ctx, attn_w_qkv[0], col_tiles=(0, q_tiles))
    kv_ctx = _qkv_proj(a_ctx, attn_w_qkv[0], col_tiles=(q_tiles, QKV // TN_MM - q_tiles))
    o_lat = _attention(qkv_lat, (qkv_lat, N_HEADS), (kv_ctx, 0), attn_sink[0], batch, band=True)
    o_ctx = _attention(q_ctx, (kv_ctx, 0), (kv_ctx, 0), attn_sink[0], batch, band=False)
    h_lat = _proj_resid(o_lat, attn_w_o[0], h_lat, mod3, 0, 2, seq, 0)
    h_ctx = _proj_resid(o_ctx, attn_w_o[0], h_ctx, mod3, 0, 2, batch * n_ctx, ctx_row)
    moe_args = (wr_pad, bias_col, moe_w_gate_up, moe_w_down)
    h_lat, a_lat = _moe_block(h_lat, norm2_g[0], mod3, 0, seq, 0, *moe_args, "next", norm1_g[1])
    h_ctx, a_ctx = _moe_block(h_ctx, norm2_g[0], mod3, 0, batch * n_ctx, ctx_row, *moe_args, "next", norm1_g[1])

    b_gate, u = _conv_in(a_lat, conv_w_in[0])
    z = _conv_gate(b_gate, u, conv_w[0], seq)
    h_lat = _proj_resid(z, conv_w_out[0], h_lat, mod3, 1, 2, seq, 0)
    h_lat = _moe_block(h_lat, norm2_g[1], mod3, 1, seq, 0, *moe_args, "final", final_g)
    return h_lat.reshape(batch, seq, D)
```
